```python
import math
import jax, jax.numpy as jnp
from jax import lax
import numpy as np

D_MODEL = 1024
BATCH = 4
SEQ = 8192
DEPTH = 1

CHUNK = 64
N_META = 16
META_PAD = (CHUNK - N_META % CHUNK) % CHUNK
RMS_EPS = 1e-6
GLA_HEADS = 4
GLA_DK = D_MODEL // 8
GLA_DV = D_MODEL // 4
GLA_GATE_RANK = 16
GLA_GATE_NORM = 16.0
GLA_QK = GLA_HEADS * GLA_DK
GLA_VW = GLA_HEADS * GLA_DV
SSD_D_INNER = 2 * D_MODEL
SSD_HEADDIM = 64
SSD_HEADS = SSD_D_INNER // SSD_HEADDIM
SSD_GROUPS = 4
SSD_HPG = SSD_HEADS // SSD_GROUPS
SSD_STATE = 128
SSD_CONV = 4
SSD_BC = SSD_GROUPS * SSD_STATE
SSD_XBC = SSD_D_INNER + 2 * SSD_BC
PEER_HEADS = 8
PEER_N_KEYS = 128
PEER_N_EXPERTS = PEER_N_KEYS * PEER_N_KEYS
PEER_KEY_DIM = 256
PEER_HALF = PEER_KEY_DIM // 2
PEER_TOPK = 16
PEER_BLOCK = 256
IN_SPLITS = (GLA_QK, GLA_QK, GLA_VW, GLA_VW, GLA_GATE_RANK, SSD_D_INNER, SSD_XBC, SSD_HEADS, D_MODEL, D_MODEL)
IN_WIDTH = GLA_QK * 2 + GLA_VW * 2 + GLA_GATE_RANK + SSD_D_INNER + SSD_XBC + SSD_HEADS + 2 * D_MODEL

kernel_name = 'hybrid_gla_ssd_peer_streaming_block'


def rmsnorm(x, w):
    xf = x.astype(jnp.float32)
    y = xf * lax.rsqrt(jnp.mean(xf * xf, axis=-1, keepdims=True) + RMS_EPS)
    return (y * w.astype(jnp.float32)).astype(x.dtype)


def split_cols(t, sizes):
    out, start = [], 0
    for s in sizes:
        out.append(t[..., start:start + s])
        start += s
    return out


def pad_front(t):
    return jnp.pad(t, [(0, 0), (META_PAD, 0)] + [(0, 0)] * (t.ndim - 2))


def to_chunks(t):
    b, lp = t.shape[:2]
    return jnp.moveaxis(t.reshape(b, lp // CHUNK, CHUNK, *t.shape[2:]), 1, 0)


def from_chunks(t):
    t = jnp.moveaxis(t, 0, 1)
    return t.reshape(t.shape[0], -1, *t.shape[3:])


def causal_dwconv(u, w, b):
    L = u.shape[1]
    up = jnp.pad(u, ((0, 0), (SSD_CONV - 1, 0), (0, 0)))
    return sum(up[:, i:i + L] * w[i] for i in range(SSD_CONV)) + b


def gla_chunk_step(S, inp):
    q, k, v, lg = inp
    G = jnp.cumsum(lg, axis=1)
    eG, eGn = jnp.exp(G), jnp.exp(-G)
    a_causal = jnp.einsum('bthk,bshk->bhts', q * eG, k * eGn)
    a_ahead = jnp.einsum('bthk,bshk->bhts', q * eGn, k * eG)
    tri = jnp.tril(jnp.ones((CHUNK, CHUNK), dtype=bool))
    att = jnp.where(tri, a_causal, a_ahead)
    o = jnp.einsum('bhts,bshv->bthv', att, v) + jnp.einsum('bthk,bhkv->bthv', q * eG, S)
    G_last = G[:, -1]
    S = S * jnp.exp(G_last)[..., None] + jnp.einsum('bshk,bshv->bhkv', k * jnp.exp(G_last[:, None] - G), v)
    return S, o


def gla_chunked(q, k, v, lg):
    S0 = jnp.zeros((q.shape[0], GLA_HEADS, GLA_DK, GLA_DV), jnp.float32)
    _, o = lax.scan(gla_chunk_step, S0, (to_chunks(q), to_chunks(k), to_chunks(v), to_chunks(lg)))
    return from_chunks(o)


def ssd_chunk_step(S, inp):
    xdt, a, Bc, Cc = inp
    acum = jnp.cumsum(a, axis=1)
    seg = jnp.exp(-jnp.abs(acum[:, :, None] - acum[:, None, :]))
    cb = jnp.einsum('btgn,bsgn->btsg', Cc, Bc)
    y = jnp.einsum('btsgr,btsg,bsgrp->btgrp', seg, cb, xdt)
    y = y + jnp.einsum('btgn,bgrpn,btgr->btgrp', Cc, S, jnp.exp(acum))
    to_end = jnp.exp(acum[:, -1:] - acum)
    S = S * jnp.exp(acum[:, -1])[..., None, None] + jnp.einsum('bsgn,bsgr,bsgrp->bgrpn', Bc, to_end, xdt)
    return S, y


def ssd_chunked(xdt, a, Bm, Cm):
    S0 = jnp.zeros((xdt.shape[0], SSD_GROUPS, SSD_HPG, SSD_HEADDIM, SSD_STATE), jnp.float32)
    _, y = lax.scan(ssd_chunk_step, S0, (to_chunks(xdt), to_chunks(a), to_chunks(Bm), to_chunks(Cm)))
    return from_chunks(y)


def hybrid_mixer(hn, w_in, gla_w_gate2, gla_b_gate, gla_norm_w, conv_w, conv_b, dt_bias, a_log, d_skip,
                 ssd_norm_w, w_up_gla, w_up_ssd, w_out):
    Bsz, L, _ = hn.shape
    f32 = jnp.float32
    proj = hn @ w_in
    q, k, v, g_out, g_low, z, xbc, dt_raw, gate_a, gate_b = split_cols(proj, IN_SPLITS)

    log_g = jax.nn.log_sigmoid((g_low @ gla_w_gate2 + gla_b_gate).astype(f32)) / GLA_GATE_NORM
    hd = lambda t, n: t.reshape(Bsz, L, GLA_HEADS, n)
    o = gla_chunked(pad_front(hd(q, GLA_DK) * GLA_DK ** -0.5), pad_front(hd(k, GLA_DK)),
                    pad_front(hd(v, GLA_DV)), pad_front(hd(log_g, GLA_DK)))[:, META_PAD:]
    o = rmsnorm(o, gla_norm_w) * jax.nn.silu(hd(g_out, GLA_DV))
    y_a = o.reshape(Bsz, L, GLA_VW).astype(hn.dtype) @ w_up_gla

    xbc = jax.nn.silu(causal_dwconv(xbc, conv_w, conv_b))
    xs, Bm, Cm = split_cols(xbc, (SSD_D_INNER, SSD_BC, SSD_BC))
    dt = jax.nn.softplus(dt_raw.astype(f32) + dt_bias)
    a = dt * (-jnp.exp(a_log.astype(f32)))
    grp = lambda t: t.reshape(Bsz, L, SSD_GROUPS, SSD_HPG)
    xs_h = xs.reshape(Bsz, L, SSD_GROUPS, SSD_HPG, SSD_HEADDIM)
    y = ssd_chunked(pad_front(xs_h * grp(dt)[..., None]), pad_front(grp(a)),
                    pad_front(Bm.reshape(Bsz, L, SSD_GROUPS, SSD_STATE)),
                    pad_front(Cm.reshape(Bsz, L, SSD_GROUPS, SSD_STATE)))[:, META_PAD:]
    y = y + d_skip.reshape(SSD_GROUPS, SSD_HPG)[:, :, None] * xs_h
    y = y.reshape(Bsz, L, SSD_D_INNER) * jax.nn.silu(z)
    y = rmsnorm(y.reshape(Bsz, L, SSD_GROUPS, SSD_D_INNER // SSD_GROUPS),
                ssd_norm_w.reshape(SSD_GROUPS, -1)).reshape(Bsz, L, SSD_D_INNER)
    y_b = y.astype(hn.dtype) @ w_up_ssd

    mixed = jax.nn.sigmoid(gate_a) * y_a + jax.nn.sigmoid(gate_b) * y_b
    return (mixed @ w_out).astype(hn.dtype)


def peer_ffn(h, w_q, sub_keys, u_tab, v_tab):
    T = h.shape[0]
    pad = (-T) % PEER_BLOCK
    hp = jnp.pad(h, ((0, pad), (0, 0))).reshape(-1, PEER_BLOCK, D_MODEL)

    def block(hb):
        n = hb.shape[0]
        q = (hb @ w_q).reshape(n, PEER_HEADS, 2, PEER_HALF)
        s = jnp.einsum('nhid,hikd->nhik', q, sub_keys).astype(jnp.float32)
        s_top, i_top = lax.top_k(s, PEER_TOPK)
        cand = s_top[:, :, 0, :, None] + s_top[:, :, 1, None, :]
        cand_idx = i_top[:, :, 0, :, None] * PEER_N_KEYS + i_top[:, :, 1, None, :]
        best, pos = lax.top_k(cand.reshape(n, PEER_HEADS, PEER_TOPK * PEER_TOPK), PEER_TOPK)
        idx = jnp.take_along_axis(cand_idx.reshape(n, PEER_HEADS, -1), pos, axis=-1)
        w = jax.nn.softmax(best, axis=-1)
        u = jnp.take(u_tab, idx, axis=0)
        act = jax.nn.gelu(jnp.einsum('nhkd,nd->nhk', u, hb).astype(jnp.float32))
        out = jnp.einsum('nhk,nhkd->nd', w * act, jnp.take(v_tab, idx, axis=0))
        return out.astype(hb.dtype)

    return lax.map(block, hp).reshape(-1, D_MODEL)[:T]


def setup_inputs(seed: int = 0) -> dict:
    key = jax.random.key(seed)
    ks = jax.random.split(key, 24)
    f32 = jnp.float32
    nrm = lambda k, shape, std: jax.random.normal(k, shape, f32) * std
    gain = lambda k, shape: 1.0 + 0.02 * jax.random.normal(k, shape, f32)
    Dp = DEPTH
    dt0 = jnp.exp(jax.random.uniform(ks[9], (Dp, SSD_HEADS), f32, math.log(1e-3), math.log(1e-1)))
    dt_bias = dt0 + jnp.log(-jnp.expm1(-dt0))
    a_log = jnp.log(jax.random.uniform(ks[10], (Dp, SSD_HEADS), f32, 1.0, 16.0))
    return {
        'x': nrm(ks[0], (BATCH, SEQ, D_MODEL), 1.0),
        'meta_tokens': nrm(ks[1], (N_META, D_MODEL), 1.0),
        'ln_mix_w': gain(ks[2], (Dp, D_MODEL)),
        'w_in': nrm(ks[3], (Dp, D_MODEL, IN_WIDTH), D_MODEL ** -0.5),
        'gla_w_gate2': nrm(ks[4], (Dp, GLA_GATE_RANK, GLA_QK), GLA_GATE_RANK ** -0.5),
        'gla_b_gate': nrm(ks[5], (Dp, GLA_QK), 0.1),
        'gla_norm_w': gain(ks[6], (Dp, GLA_DV)),
        'ssd_conv_w': nrm(ks[7], (Dp, SSD_CONV, SSD_XBC), SSD_CONV ** -0.5),
        'ssd_conv_b': nrm(ks[8], (Dp, SSD_XBC), 0.02),
        'ssd_dt_bias': dt_bias,
        'ssd_a_log': a_log,
        'ssd_d': gain(ks[11], (Dp, SSD_HEADS)),
        'ssd_norm_w': gain(ks[12], (Dp, SSD_D_INNER)),
        'w_up_gla': nrm(ks[13], (Dp, GLA_VW, D_MODEL), GLA_VW ** -0.5),
        'w_up_ssd': nrm(ks[14], (Dp, SSD_D_INNER, D_MODEL), SSD_D_INNER ** -0.5),
        'w_out': nrm(ks[15], (Dp, D_MODEL, D_MODEL), D_MODEL ** -0.5),
        'ln_ffn_w': gain(ks[16], (Dp, D_MODEL)),
        'peer_w_q': nrm(ks[17], (Dp, D_MODEL, PEER_HEADS * PEER_KEY_DIM), D_MODEL ** -0.5),
        'peer_sub_keys': nrm(ks[18], (Dp, PEER_HEADS, 2, PEER_N_KEYS, PEER_HALF), PEER_HALF ** -0.5),
        'peer_u': nrm(ks[19], (Dp, PEER_N_EXPERTS, D_MODEL), D_MODEL ** -0.5),
        'peer_v': nrm(ks[20], (Dp, PEER_N_EXPERTS, D_MODEL), PEER_HEADS ** -0.5),
        'ln_final_w': gain(ks[21], (D_MODEL,)),
    }


def reference(x, meta_tokens, ln_mix_w, w_in, gla_w_gate2, gla_b_gate, gla_norm_w, ssd_conv_w, ssd_conv_b,
              ssd_dt_bias, ssd_a_log, ssd_d, ssd_norm_w, w_up_gla, w_up_ssd, w_out, ln_ffn_w,
              peer_w_q, peer_sub_keys, peer_u, peer_v, ln_final_w):
    Bsz = x.shape[0]
    meta = jnp.broadcast_to(meta_tokens[None].astype(x.dtype), (Bsz, N_META, D_MODEL))
    h = jnp.concatenate([meta, x], axis=1)
    for l in range(DEPTH):
        hn = rmsnorm(h, ln_mix_w[l])
        h = h + hybrid_mixer(hn, w_in[l], gla_w_gate2[l], gla_b_gate[l], gla_norm_w[l], ssd_conv_w[l],
                             ssd_conv_b[l], ssd_dt_bias[l], ssd_a_log[l], ssd_d[l], ssd_norm_w[l],
                             w_up_gla[l], w_up_ssd[l], w_out[l])
        if l == DEPTH - 1:
            h = h[:, N_META:]
        hn = rmsnorm(h, ln_ffn_w[l])
        ffn = peer_ffn(hn.reshape(-1, D_MODEL), peer_w_q[l], peer_sub_keys[l], peer_u[l], peer_v[l])
        h = h + ffn.reshape(h.shape)
    return rmsnorm(h, ln_final_w)
```

```python
import functools
import math

import jax
import jax.numpy as jnp
from jax import lax
from jax.experimental import pallas as pl
from jax.experimental.pallas import tpu as pltpu

F32 = jnp.float32
BF16 = jnp.bfloat16

D_MODEL = 1024
CHUNK = 64
N_META = 16
META_PAD = CHUNK - N_META
RMS_EPS = 1e-6
GLA_HEADS = 4
GLA_DK = 128
GLA_DV = 256
GLA_GATE_RANK = 16
GLA_GATE_NORM = 16.0
GLA_QK = 512
GLA_VW = 1024
SSD_D_INNER = 2048
SSD_HEADDIM = 64
SSD_HEADS = 32
SSD_GROUPS = 4
SSD_HPG = 8
SSD_STATE = 128
SSD_CONV = 4
SSD_BC = 512
SSD_XBC = 3072
SSD_GW = SSD_D_INNER // SSD_GROUPS
PEER_HEADS = 8
PEER_N_KEYS = 128
PEER_HALF = 128
PEER_TOPK = 16
PEER_SLOTS = PEER_HEADS * PEER_TOPK

COL_Q, COL_K, COL_V, COL_Z, COL_GOUT, COL_GA, COL_XBC, COL_GB = 0, 512, 1024, 2048, 4096, 5120, 6144, 9216
MAIN_W = 10240
SMALL_W = 128
SMALL_GLOW = 0
SMALL_DT = 16

V7X_VMEM_LIMIT = 56 * 1024 * 1024


def _dot(a, b):
    return jnp.dot(a.astype(BF16), b.astype(BF16), preferred_element_type=F32)


def _dot_nt(a, b):
    return lax.dot_general(a.astype(BF16), b.astype(BF16), (((1,), (1,)), ((), ())),
                           preferred_element_type=F32)


def _dot_tn(a, b):
    return lax.dot_general(a.astype(BF16), b.astype(BF16), (((0,), (0,)), ((), ())),
                           preferred_element_type=F32)


def _split3(x):
    x1 = x.astype(BF16)
    r = x - x1.astype(F32)
    x2 = r.astype(BF16)
    x3 = (r - x2.astype(F32)).astype(BF16)
    return x1, x2, x3


def _dot_exact_l(m, x):
    x1, x2, x3 = _split3(x)
    return (jnp.dot(m, x1, preferred_element_type=F32) + jnp.dot(m, x2, preferred_element_type=F32)
            + jnp.dot(m, x3, preferred_element_type=F32))


def _dot_exact_r(x, m):
    x1, x2, x3 = _split3(x)
    return (jnp.dot(x1, m, preferred_element_type=F32) + jnp.dot(x2, m, preferred_element_type=F32)
            + jnp.dot(x3, m, preferred_element_type=F32))


def _sigmoid(x):
    return 1.0 / (1.0 + jnp.exp(-x))


def _silu(x):
    return x * _sigmoid(x)


def _softplus(x):
    return jnp.maximum(x, 0.0) + jnp.log1p(jnp.exp(-jnp.abs(x)))


def _inproj_kernel(x_ref, lnw_ref, w_ref, ws_ref, o_ref, os_ref, hn_ref):
    @pl.when(pl.program_id(1) == 0)
    def _():
        x = x_ref[...]
        ms = jnp.mean(x * x, axis=-1, keepdims=True)
        hn_ref[...] = (x * lax.rsqrt(ms + RMS_EPS) * lnw_ref[...]).astype(BF16)
        os_ref[...] = jnp.dot(hn_ref[...], ws_ref[...], preferred_element_type=F32)

    o_ref[...] = jnp.dot(hn_ref[...], w_ref[...], preferred_element_type=F32)


def _inproj(x2d, lnw, w_main, w_small):
    rows = x2d.shape[0]
    tm = min(1024, rows)
    tn = 1024
    return pl.pallas_call(
        _inproj_kernel,
        grid=(rows // tm, MAIN_W // tn),
        in_specs=[
            pl.BlockSpec((tm, D_MODEL), lambda i, j: (i, 0)),
            pl.BlockSpec((1, D_MODEL), lambda i, j: (0, 0)),
            pl.BlockSpec((D_MODEL, tn), lambda i, j: (0, j)),
            pl.BlockSpec((D_MODEL, SMALL_W), lambda i, j: (0, 0)),
        ],
        out_specs=[
            pl.BlockSpec((tm, tn), lambda i, j: (i, j)),
            pl.BlockSpec((tm, SMALL_W), lambda i, j: (i, 0)),
        ],
        out_shape=[
            jax.ShapeDtypeStruct((rows, MAIN_W), F32),
            jax.ShapeDtypeStruct((rows, SMALL_W), F32),
        ],
        scratch_shapes=[pltpu.VMEM((tm, D_MODEL), BF16)],
        compiler_params=pltpu.CompilerParams(
            dimension_semantics=("arbitrary", "arbitrary"), vmem_limit_bytes=V7X_VMEM_LIMIT),
        name="inproj",
    )(x2d, lnw, w_main, w_small)


def _gla_kernel(q_ref, k_ref, v_ref, g_ref, sm_ref, w2_ref, bg_ref, nw_ref, s0_ref, tri_ref,
                og_ref, sfin_ref, s_ref, *, cb, mask_rows):
    c = pl.program_id(1)

    @pl.when(c == 0)
    def _():
        s_ref[...] = s0_ref[...]

    tri = tri_ref[...]
    tri_b = (lax.broadcasted_iota(jnp.int32, (CHUNK, CHUNK), 1)
             <= lax.broadcasted_iota(jnp.int32, (CHUNK, CHUNK), 0))
    scale = GLA_DK ** -0.5

    def chunk_step(ci, carry):
        rs = pl.ds(pl.multiple_of(ci * CHUNK, CHUNK), CHUNK)
        lg = _dot(sm_ref[rs, :], w2_ref[...]) + bg_ref[...]
        lg = (jnp.minimum(lg, 0.0) - jnp.log1p(jnp.exp(-jnp.abs(lg)))) * (1.0 / GLA_GATE_NORM)
        if mask_rows:
            keep = lax.broadcasted_iota(jnp.int32, (CHUNK, GLA_QK), 0) >= mask_rows
            lg = jnp.where(keep, lg, 0.0)
        gcum = _dot_exact_l(tri, lg)
        for h in range(GLA_HEADS):
            ks = slice(h * GLA_DK, (h + 1) * GLA_DK)
            vs = slice(h * GLA_DV, (h + 1) * GLA_DV)
            qh = q_ref[rs, ks] * scale
            kh = k_ref[rs, ks]
            vh = v_ref[rs, vs]
            if mask_rows:
                keep_k = lax.broadcasted_iota(jnp.int32, (CHUNK, GLA_DK), 0) >= mask_rows
                keep_v = lax.broadcasted_iota(jnp.int32, (CHUNK, GLA_DV), 0) >= mask_rows
                qh = jnp.where(keep_k, qh, 0.0)
                kh = jnp.where(keep_k, kh, 0.0)
                vh = jnp.where(keep_v, vh, 0.0)
            gh = gcum[:, ks]
            eg = jnp.exp(gh)
            egn = jnp.exp(-gh)
            qe = qh * eg
            a_causal = _dot_nt(qe, kh * egn)
            a_ahead = _dot_nt(qh * egn, kh * eg)
            att = jnp.where(tri_b, a_causal, a_ahead)
            st = s_ref[h]
            o = _dot(att, vh) + _dot_nt(qe, st)
            glast = gh[CHUNK - 1:CHUNK, :]
            kd = kh * jnp.exp(glast - gh)
            s_ref[h] = st * jnp.exp(glast) + _dot_tn(vh, kd)
            ms = jnp.mean(o * o, axis=-1, keepdims=True)
            on = o * lax.rsqrt(ms + RMS_EPS) * nw_ref[...]
            og_ref[rs, vs] = (on * _silu(g_ref[rs, vs])).astype(og_ref.dtype)
        return carry

    lax.fori_loop(0, cb, chunk_step, 0)

    @pl.when(c == pl.num_programs(1) - 1)
    def _():
        sfin_ref[0] = s_ref[...]


def _gla(proj, small, w2p, bgate, normw, s0, tri, *, batch, seq, cb, mask_rows):
    nb = seq // (cb * CHUNK)
    rb = cb * CHUNK
    row = lambda b, c: b * nb + c
    kern = functools.partial(_gla_kernel, cb=cb, mask_rows=mask_rows)
    return pl.pallas_call(
        kern,
        grid=(batch, nb),
        in_specs=[
            pl.BlockSpec((rb, GLA_QK), lambda b, c: (row(b, c), COL_Q // GLA_QK)),
            pl.BlockSpec((rb, GLA_QK), lambda b, c: (row(b, c), COL_K // GLA_QK)),
            pl.BlockSpec((rb, GLA_VW), lambda b, c: (row(b, c), COL_V // GLA_VW)),
            pl.BlockSpec((rb, GLA_VW), lambda b, c: (row(b, c), COL_GOUT // GLA_VW)),
            pl.BlockSpec((rb, SMALL_W), lambda b, c: (row(b, c), 0)),
            pl.BlockSpec((SMALL_W, GLA_QK), lambda b, c: (0, 0)),
            pl.BlockSpec((1, GLA_QK), lambda b, c: (0, 0)),
            pl.BlockSpec((1, GLA_DV), lambda b, c: (0, 0)),
            pl.BlockSpec((GLA_HEADS, GLA_DV, GLA_DK), lambda b, c: (0, 0, 0)),
            pl.BlockSpec((CHUNK, CHUNK), lambda b, c: (0, 0)),
        ],
        out_specs=[
            pl.BlockSpec((rb, GLA_VW), lambda b, c: (row(b, c), 0)),
            pl.BlockSpec((1, GLA_HEADS, GLA_DV, GLA_DK), lambda b, c: (b, 0, 0, 0)),
        ],
        out_shape=[
            jax.ShapeDtypeStruct((batch * seq, GLA_VW), BF16),
            jax.ShapeDtypeStruct((batch, GLA_HEADS, GLA_DV, GLA_DK), F32),
        ],
        scratch_shapes=[pltpu.VMEM((GLA_HEADS, GLA_DV, GLA_DK), F32)],
        compiler_params=pltpu.CompilerParams(
            dimension_semantics=("arbitrary", "arbitrary"), vmem_limit_bytes=V7X_VMEM_LIMIT),
        name="gla_scan",
    )(proj, proj, proj, proj, small, w2p, bgate, normw, s0, tri)


def _ssd_kernel(xbc_ref, z_ref, sm_ref, cw_ref, cbias_ref, dtb_ref, alog_ref, dsk_ref, nw_ref,
                s0_ref, u0_ref, tri_ref, ones_ref, exp_ref, dm_ref, bmask_ref,
                yg_ref, sfin_ref, ufin_ref, s_ref, prev_ref, *, cb, mask_rows):
    c = pl.program_id(1)

    @pl.when(c == 0)
    def _():
        s_ref[...] = s0_ref[...]
        prev_ref[...] = u0_ref[...]

    tri = tri_ref[...]
    ones64 = ones_ref[...]
    expand = exp_ref[...]
    row_x = lax.broadcasted_iota(jnp.int32, (CHUNK, SSD_XBC), 0)

    def chunk_step(ci, carry):
        rs = pl.ds(pl.multiple_of(ci * CHUNK, CHUNK), CHUNK)
        u = xbc_ref[rs, :]
        prev = prev_ref[...]
        conv = u * cw_ref[SSD_CONV - 1:SSD_CONV, :] + cbias_ref[...]
        for j in range(1, SSD_CONV):
            shifted = jnp.where(row_x < j, pltpu.roll(prev, j, 0), pltpu.roll(u, j, 0))
            conv = conv + shifted * cw_ref[SSD_CONV - 1 - j:SSD_CONV - j, :]
        prev_ref[...] = u
        xc = _silu(conv)
        dt = _softplus(sm_ref[rs, :] + dtb_ref[...])
        if mask_rows:
            xc = jnp.where(row_x >= mask_rows, xc, 0.0)
            dt = jnp.where(lax.broadcasted_iota(jnp.int32, (CHUNK, SMALL_W), 0) >= mask_rows, dt, 0.0)
        a = -dt * jnp.exp(alog_ref[...])
        xs = xc[:, :SSD_D_INNER]
        bm = xc[:, SSD_D_INNER:SSD_D_INNER + SSD_BC]
        cm = xc[:, SSD_D_INNER + SSD_BC:]
        acum = _dot_exact_l(tri, a)
        colb = _dot_exact_r(acum, expand)
        dtx = _dot_exact_r(dt, expand)
        rowb = _dot_exact_l(ones64, colb * dm_ref[...])
        seg = jnp.exp(-jnp.abs(colb - rowb))
        clast = colb[CHUNK - 1:CHUNK, :]
        eac = jnp.exp(colb)
        toend = jnp.exp(clast - colb)
        dec = jnp.exp(clast)
        xdt = xs * dtx
        for g in range(SSD_GROUPS):
            gs = slice(g * SSD_GW, (g + 1) * SSD_GW)
            ns = slice(g * SSD_STATE, (g + 1) * SSD_STATE)
            bg = bm[:, ns]
            cg = cm[:, ns]
            xg = xdt[:, gs]
            cbx = _dot_nt(cg, jnp.concatenate([bg] * SSD_HPG, axis=0))
            sc = seg[:, gs] * cbx
            bd = jnp.concatenate([xg.astype(BF16)] * SSD_HPG, axis=0) * bmask_ref[...]
            st = s_ref[g]
            y = _dot(sc, bd) + _dot(cg, st) * eac[:, gs]
            s_ref[g] = st * dec[:, gs] + _dot_tn(bg, xg * toend[:, gs])
            y = y + dsk_ref[:, gs] * xs[:, gs]
            y = y * _silu(z_ref[rs, gs])
            ms = jnp.mean(y * y, axis=-1, keepdims=True)
            yg_ref[rs, gs] = (y * lax.rsqrt(ms + RMS_EPS) * nw_ref[:, gs]).astype(yg_ref.dtype)
        return carry

    lax.fori_loop(0, cb, chunk_step, 0)

    @pl.when(c == pl.num_programs(1) - 1)
    def _():
        sfin_ref[0] = s_ref[...]
        ufin_ref[0] = prev_ref[...]


def _ssd(proj, small, cw, cbias, dtb, alog, dsk, normw, s0, u0, consts, *, batch, seq, cb, mask_rows):
    nb = seq // (cb * CHUNK)
    rb = cb * CHUNK
    row = lambda b, c: b * nb + c
    tri, ones64, expand, dmask, bmask = consts
    kern = functools.partial(_ssd_kernel, cb=cb, mask_rows=mask_rows)
    full2 = lambda shape: pl.BlockSpec(shape, lambda b, c: (0, 0))
    return pl.pallas_call(
        kern,
        grid=(batch, nb),
        in_specs=[
            pl.BlockSpec((rb, SSD_XBC), lambda b, c: (row(b, c), COL_XBC // SSD_XBC)),
            pl.BlockSpec((rb, SSD_D_INNER), lambda b, c: (row(b, c), COL_Z // SSD_D_INNER)),
            pl.BlockSpec((rb, SMALL_W), lambda b, c: (row(b, c), 0)),
            full2((SSD_CONV, SSD_XBC)),
            full2((1, SSD_XBC)),
            full2((1, SMALL_W)),
            full2((1, SMALL_W)),
            full2((1, SSD_D_INNER)),
            full2((1, SSD_D_INNER)),
            pl.BlockSpec((SSD_GROUPS, SSD_STATE, SSD_GW), lambda b, c: (0, 0, 0)),
            full2((CHUNK, SSD_XBC)),
            full2((CHUNK, CHUNK)),
            full2((CHUNK, CHUNK)),
            full2((SMALL_W, SSD_D_INNER)),
            full2((CHUNK, SSD_D_INNER)),
            full2((SSD_GW, SSD_GW)),
        ],
        out_specs=[
            pl.BlockSpec((rb, SSD_D_INNER), lambda b, c: (row(b, c), 0)),
            pl.BlockSpec((1, SSD_GROUPS, SSD_STATE, SSD_GW), lambda b, c: (b, 0, 0, 0)),
            pl.BlockSpec((1, CHUNK, SSD_XBC), lambda b, c: (b, 0, 0)),
        ],
        out_shape=[
            jax.ShapeDtypeStruct((batch * seq, SSD_D_INNER), BF16),
            jax.ShapeDtypeStruct((batch, SSD_GROUPS, SSD_STATE, SSD_GW), F32),
            jax.ShapeDtypeStruct((batch, CHUNK, SSD_XBC), F32),
        ],
        scratch_shapes=[pltpu.VMEM((SSD_GROUPS, SSD_STATE, SSD_GW), F32),
                        pltpu.VMEM((CHUNK, SSD_XBC), F32)],
        compiler_params=pltpu.CompilerParams(
            dimension_semantics=("arbitrary", "arbitrary"), vmem_limit_bytes=V7X_VMEM_LIMIT),
        name="ssd_scan",
    )(proj, proj, small, cw, cbias, dtb, alog, dsk, normw, s0, u0, tri, ones64, expand, dmask, bmask)


def _merge_kernel(og_ref, yg_ref, ga_ref, gb_ref, x_ref, wa_ref, wb_ref, wo_ref, lnw_ref, h_ref, hn_ref):
    ya = jnp.dot(og_ref[...], wa_ref[...], preferred_element_type=F32)
    yb = jnp.dot(yg_ref[...], wb_ref[...], preferred_element_type=F32)
    mixed = _sigmoid(ga_ref[...]) * ya + _sigmoid(gb_ref[...]) * yb
    h = x_ref[...] + _dot(mixed, wo_ref[...])
    h_ref[...] = h
    ms = jnp.mean(h * h, axis=-1, keepdims=True)
    hn_ref[...] = (h * lax.rsqrt(ms + RMS_EPS) * lnw_ref[...]).astype(BF16)


def _merge(og, yg, proj, x2d, wa, wb, wo, lnw):
    rows = x2d.shape[0]
    tm = min(512, rows)
    full = lambda shape: pl.BlockSpec(shape, lambda i: (0, 0))
    return pl.pallas_call(
        _merge_kernel,
        grid=(rows // tm,),
        in_specs=[
            pl.BlockSpec((tm, GLA_VW), lambda i: (i, 0)),
            pl.BlockSpec((tm, SSD_D_INNER), lambda i: (i, 0)),
            pl.BlockSpec((tm, D_MODEL), lambda i: (i, COL_GA // D_MODEL)),
            pl.BlockSpec((tm, D_MODEL), lambda i: (i, COL_GB // D_MODEL)),
            pl.BlockSpec((tm, D_MODEL), lambda i: (i, 0)),
            full((GLA_VW, D_MODEL)),
            full((SSD_D_INNER, D_MODEL)),
            full((D_MODEL, D_MODEL)),
            full((1, D_MODEL)),
        ],
        out_specs=[
            pl.BlockSpec((tm, D_MODEL), lambda i: (i, 0)),
            pl.BlockSpec((tm, D_MODEL), lambda i: (i, 0)),
        ],
        out_shape=[
            jax.ShapeDtypeStruct((rows, D_MODEL), F32),
            jax.ShapeDtypeStruct((rows, D_MODEL), BF16),
        ],
        compiler_params=pltpu.CompilerParams(
            dimension_semantics=("arbitrary",), vmem_limit_bytes=V7X_VMEM_LIMIT),
        name="merge",
    )(og, yg, proj, proj, x2d, wa, wb, wo, lnw)


PEER_LB = 256


def _topk_rows(vals, k, out_v_ref, out_i_ref, row0):
    n = vals.shape[0]
    ridx = lax.broadcasted_iota(jnp.int32, vals.shape, 0)
    for r in range(k):
        m = jnp.max(vals, axis=0, keepdims=True)
        idx = jnp.min(jnp.where(vals == m, ridx, n), axis=0, keepdims=True)
        out_v_ref[row0 + r:row0 + r + 1, :] = m
        out_i_ref[row0 + r:row0 + r + 1, :] = idx
        vals = jnp.where(ridx == idx, -jnp.inf, vals)


def _select_kernel(hn_ref, wq_ref, keys_ref, i_ref, j_ref, w_ref,
                   tv_ref, ti_ref, cand_ref, bv_ref, bp_ref, oi_ref, oj_ref, ow_ref):
    h = pl.program_id(1)
    hn = hn_ref[...]
    for half in range(2):
        qt = lax.dot_general(wq_ref[half * PEER_HALF:(half + 1) * PEER_HALF, :], hn,
                             (((1,), (1,)), ((), ())), preferred_element_type=F32)
        s = jnp.dot(keys_ref[0, half], qt.astype(BF16), preferred_element_type=F32)
        _topk_rows(s, PEER_TOPK, tv_ref, ti_ref, half * PEER_TOPK)
    tv1 = tv_ref[PEER_TOPK:2 * PEER_TOPK, :]
    for a in range(PEER_TOPK):
        cand_ref[a * PEER_TOPK:(a + 1) * PEER_TOPK, :] = tv_ref[a:a + 1, :] + tv1
    _topk_rows(cand_ref[...], PEER_TOPK, bv_ref, bp_ref, 0)
    best = bv_ref[...]
    pos = bp_ref[...]
    a_sel = pos >> 4
    b_sel = pos & (PEER_TOPK - 1)
    isel = jnp.zeros_like(pos)
    jsel = jnp.zeros_like(pos)
    for a in range(PEER_TOPK):
        isel = jnp.where(a_sel == a, ti_ref[a:a + 1, :], isel)
        jsel = jnp.where(b_sel == a, ti_ref[PEER_TOPK + a:PEER_TOPK + a + 1, :], jsel)
    e = jnp.exp(best - best[0:1, :])
    wgt = e / jnp.sum(e, axis=0, keepdims=True)
    hs = pl.ds(pl.multiple_of(h * PEER_TOPK, PEER_TOPK), PEER_TOPK)
    oi_ref[hs, :] = isel.astype(F32)
    oj_ref[hs, :] = jsel.astype(F32)
    ow_ref[hs, :] = wgt

    @pl.when(h == PEER_HEADS - 1)
    def _():
        i_ref[...] = oi_ref[...].T
        j_ref[...] = oj_ref[...].T
        w_ref[...] = ow_ref[...].T


def _select(hn2, wq_t, keys):
    t = hn2.shape[0]
    tn = PEER_LB
    out = jax.ShapeDtypeStruct((t, PEER_SLOTS), F32)
    return pl.pallas_call(
        _select_kernel,
        grid=(t // tn, PEER_HEADS),
        in_specs=[
            pl.BlockSpec((tn, D_MODEL), lambda i, h: (i, 0)),
            pl.BlockSpec((2 * PEER_HALF, D_MODEL), lambda i, h: (h, 0)),
            pl.BlockSpec((1, 2, PEER_N_KEYS, PEER_HALF), lambda i, h: (h, 0, 0, 0)),
        ],
        out_specs=[pl.BlockSpec((tn, PEER_SLOTS), lambda i, h: (i, 0))] * 3,
        out_shape=[out, out, out],
        scratch_shapes=[
            pltpu.VMEM((2 * PEER_TOPK, tn), F32),
            pltpu.VMEM((2 * PEER_TOPK, tn), jnp.int32),
            pltpu.VMEM((PEER_TOPK * PEER_TOPK, tn), F32),
            pltpu.VMEM((PEER_TOPK, tn), F32),
            pltpu.VMEM((PEER_TOPK, tn), jnp.int32),
            pltpu.VMEM((PEER_SLOTS, tn), F32),
            pltpu.VMEM((PEER_SLOTS, tn), F32),
            pltpu.VMEM((PEER_SLOTS, tn), F32),
        ],
        compiler_params=pltpu.CompilerParams(
            dimension_semantics=("arbitrary", "arbitrary"), vmem_limit_bytes=V7X_VMEM_LIMIT),
        name="peer_select",
    )(hn2, wq_t, keys)


PEER_TN = 512
PEER_EB = 1024
PEER_IH = 64
PEER_PITCH = PEER_TN + 8


def _gelu_tanh(x):
    return 0.5 * x * (1.0 + jnp.tanh(math.sqrt(2.0 / math.pi) * (x + 0.044715 * (x * x * x))))


def _expert_kernel(hn_ref, i_ref, j_ref, w_ref, h_ref, u_ref, v_ref, lnw_ref, o_ref, acc_ref, r_ref,
                   *, tn, pitch):
    e = pl.program_id(1)
    steps_per_half = (PEER_IH * PEER_N_KEYS) // PEER_EB
    ipb = PEER_EB // PEER_N_KEYS

    @pl.when(e == 0)
    def _():
        acc_ref[...] = jnp.zeros_like(acc_ref)

    @pl.when(e % steps_per_half == 0)
    def _():
        i_base = (e // steps_per_half) * PEER_IH
        sub = (lax.broadcasted_iota(jnp.int32, (PEER_IH, PEER_SLOTS), 0) + i_base).astype(F32)
        subj = lax.broadcasted_iota(jnp.int32, (PEER_N_KEYS, PEER_SLOTS), 0).astype(F32)

        def body(n, carry):
            irow = i_ref[pl.ds(n, 1), :]
            jrow = j_ref[pl.ds(n, 1), :]
            wrow = w_ref[pl.ds(n, 1), :]
            w_hi = wrow.astype(BF16)
            w_lo = (wrow - w_hi.astype(F32)).astype(BF16)
            hit = sub == irow
            q_hi = jnp.where(hit, w_hi.astype(F32), 0.0).astype(BF16)
            q_lo = jnp.where(hit, w_lo.astype(F32), 0.0).astype(BF16)
            p = jnp.where(subj == jrow, 1.0, 0.0).astype(BF16)
            r = (lax.dot_general(q_hi, p, (((1,), (1,)), ((), ())), preferred_element_type=F32)
                 + lax.dot_general(q_lo, p, (((1,), (1,)), ((), ())), preferred_element_type=F32))
            r_ref[pl.ds(n, PEER_IH, stride=pitch), :] = r
            return carry

        lax.fori_loop(0, tn, body, 0)

    i_loc = (e % steps_per_half) * ipb
    act = lax.dot_general(hn_ref[...], u_ref[...], (((1,), (1,)), ((), ())),
                          preferred_element_type=F32)
    wblk = jnp.concatenate(
        [r_ref[pl.ds(pl.multiple_of((i_loc + ii) * pitch, 8), tn), :] for ii in range(ipb)], axis=1)
    g = (_gelu_tanh(act) * wblk).astype(BF16)
    acc_ref[...] += jnp.dot(g, v_ref[...], preferred_element_type=F32)

    @pl.when(e == pl.num_programs(1) - 1)
    def _():
        h = h_ref[...] + acc_ref[...]
        ms = jnp.mean(h * h, axis=-1, keepdims=True)
        o_ref[...] = h * lax.rsqrt(ms + RMS_EPS) * lnw_ref[...]


def _experts(hn2, isel, jsel, wsel, h2, u_tab, v_tab, lnw):
    t = hn2.shape[0]
    tn = min(PEER_TN, t)
    pitch = tn + 8
    n_exp = u_tab.shape[0]
    kern = functools.partial(_expert_kernel, tn=tn, pitch=pitch)
    return pl.pallas_call(
        kern,
        grid=(t // tn, n_exp // PEER_EB),
        in_specs=[
            pl.BlockSpec((tn, D_MODEL), lambda i, e: (i, 0)),
            pl.BlockSpec((tn, PEER_SLOTS), lambda i, e: (i, 0)),
            pl.BlockSpec((tn, PEER_SLOTS), lambda i, e: (i, 0)),
            pl.BlockSpec((tn, PEER_SLOTS), lambda i, e: (i, 0)),
            pl.BlockSpec((tn, D_MODEL), lambda i, e: (i, 0)),
            pl.BlockSpec((PEER_EB, D_MODEL), lambda i, e: (e, 0)),
            pl.BlockSpec((PEER_EB, D_MODEL), lambda i, e: (e, 0)),
            pl.BlockSpec((1, D_MODEL), lambda i, e: (0, 0)),
        ],
        out_specs=pl.BlockSpec((tn, D_MODEL), lambda i, e: (i, 0)),
        out_shape=jax.ShapeDtypeStruct((t, D_MODEL), F32),
        scratch_shapes=[pltpu.VMEM((tn, D_MODEL), F32),
                        pltpu.VMEM((PEER_IH * pitch, PEER_N_KEYS), F32)],
        compiler_params=pltpu.CompilerParams(
            dimension_semantics=("arbitrary", "arbitrary"), vmem_limit_bytes=V7X_VMEM_LIMIT),
        name="peer_experts",
    )(hn2, isel, jsel, wsel, h2, u_tab, v_tab, lnw)


def _constants():
    t = jnp.arange(CHUNK)
    tri = (t[None, :] <= t[:, None]).astype(BF16)
    ones64 = jnp.ones((CHUNK, CHUNK), BF16)
    lane = jnp.arange(SSD_D_INNER)
    expand = (jnp.arange(SMALL_W)[:, None] == SMALL_DT + lane[None, :] // SSD_HEADDIM).astype(BF16)
    dmask = (t[:, None] == lane[None, :] % CHUNK).astype(F32)
    gl = jnp.arange(SSD_GW)
    bmask = (gl[:, None] // CHUNK == gl[None, :] // SSD_HEADDIM).astype(BF16)
    return tri, ones64, expand, dmask, bmask


def _pack_in_weights(w_in):
    o = 0
    parts = {}
    for name, width in (("q", GLA_QK), ("k", GLA_QK), ("v", GLA_VW), ("gout", GLA_VW), ("glow", GLA_GATE_RANK),
                        ("z", SSD_D_INNER), ("xbc", SSD_XBC), ("dt", SSD_HEADS), ("ga", D_MODEL), ("gb", D_MODEL)):
        parts[name] = w_in[:, o:o + width]
        o += width
    main = jnp.concatenate([parts[n] for n in ("q", "k", "v", "z", "gout", "ga", "xbc", "gb")], axis=1)
    small = jnp.concatenate(
        [parts["glow"], parts["dt"], jnp.zeros((D_MODEL, SMALL_W - GLA_GATE_RANK - SSD_HEADS), w_in.dtype)], axis=1)
    return main.astype(BF16), small.astype(BF16)


def _pad_small(vec, offset):
    out = jnp.zeros((1, SMALL_W), F32)
    return out.at[0, offset:offset + vec.shape[0]].set(vec.astype(F32))


def kernel(x, meta_tokens, ln_mix_w, w_in, gla_w_gate2, gla_b_gate, gla_norm_w, ssd_conv_w, ssd_conv_b,
           ssd_dt_bias, ssd_a_log, ssd_d, ssd_norm_w, w_up_gla, w_up_ssd, w_out, ln_ffn_w,
           peer_w_q, peer_sub_keys, peer_u, peer_v, ln_final_w):
    bsz, seq, _ = x.shape
    assert seq % (4 * CHUNK) == 0
    l = 0
    consts = _constants()
    tri = consts[0]

    w_main, w_small = _pack_in_weights(w_in[l])
    lnw = ln_mix_w[l].reshape(1, D_MODEL)
    w2p = jnp.zeros((SMALL_W, GLA_QK), F32).at[SMALL_GLOW:SMALL_GLOW + GLA_GATE_RANK].set(gla_w_gate2[l]).astype(BF16)
    bgate = gla_b_gate[l].reshape(1, GLA_QK)
    gnw = gla_norm_w[l].reshape(1, GLA_DV)
    cw = ssd_conv_w[l]
    cbias = ssd_conv_b[l].reshape(1, SSD_XBC)
    dtb = _pad_small(ssd_dt_bias[l], SMALL_DT)
    alog = _pad_small(ssd_a_log[l], SMALL_DT)
    dsk = jnp.repeat(ssd_d[l].astype(F32), SSD_HEADDIM).reshape(1, SSD_D_INNER)
    snw = ssd_norm_w[l].reshape(1, SSD_D_INNER)

    meta_rows = jnp.concatenate([jnp.zeros((META_PAD, D_MODEL), F32), meta_tokens.astype(F32)], axis=0)
    proj_m, small_m = _inproj(meta_rows, lnw, w_main, w_small)
    gla_s0 = jnp.zeros((GLA_HEADS, GLA_DV, GLA_DK), F32)
    ssd_s0 = jnp.zeros((SSD_GROUPS, SSD_STATE, SSD_GW), F32)
    u0 = jnp.zeros((CHUNK, SSD_XBC), F32)
    _, gla_s1 = _gla(proj_m, small_m, w2p, bgate, gnw, gla_s0, tri, batch=1, seq=CHUNK, cb=1, mask_rows=META_PAD)
    _, ssd_s1, u1 = _ssd(proj_m, small_m, cw, cbias, dtb, alog, dsk, snw, ssd_s0, u0, consts,
                         batch=1, seq=CHUNK, cb=1, mask_rows=META_PAD)

    x2d = x.reshape(bsz * seq, D_MODEL)
    proj, small = _inproj(x2d, lnw, w_main, w_small)
    og, _ = _gla(proj, small, w2p, bgate, gnw, gla_s1[0], tri, batch=bsz, seq=seq, cb=4, mask_rows=0)
    yg, _, _ = _ssd(proj, small, cw, cbias, dtb, alog, dsk, snw, ssd_s1[0], u1[0], consts,
                    batch=bsz, seq=seq, cb=4, mask_rows=0)
    h2, hn2 = _merge(og, yg, proj, x2d, w_up_gla[l].astype(BF16), w_up_ssd[l].astype(BF16),
                     w_out[l].astype(BF16), ln_ffn_w[l].reshape(1, D_MODEL))

    wq_t = peer_w_q[l].T.astype(BF16)
    keys = peer_sub_keys[l].astype(BF16)
    isel, jsel, wsel = _select(hn2, wq_t, keys)
    out = _experts(hn2, isel, jsel, wsel, h2, peer_u[l].astype(BF16), peer_v[l].astype(BF16),
                   ln_final_w.reshape(1, D_MODEL))
    return out.reshape(bsz, seq, D_MODEL)
```

```python
import functools
import math

import jax
import jax.numpy as jnp
from jax import lax
from jax.experimental import pallas as pl
from jax.experimental.pallas import tpu as pltpu

F32 = jnp.float32
BF16 = jnp.bfloat16

D_MODEL = 1024
CHUNK = 64
N_META = 16
META_PAD = CHUNK - N_META
RMS_EPS = 1e-6
GLA_HEADS = 4
GLA_DK = 128
GLA_DV = 256
GLA_GATE_RANK = 16
GLA_GATE_NORM = 16.0
GLA_QK = 512
GLA_VW = 1024
SSD_D_INNER = 2048
SSD_HEADDIM = 64
SSD_HEADS = 32
SSD_GROUPS = 4
SSD_HPG = 8
SSD_STATE = 128
SSD_CONV = 4
SSD_BC = 512
SSD_XBC = 3072
SSD_GW = SSD_D_INNER // SSD_GROUPS
PEER_HEADS = 8
PEER_N_KEYS = 128
PEER_HALF = 128
PEER_TOPK = 16
PEER_SLOTS = PEER_HEADS * PEER_TOPK

COL_Q, COL_K, COL_V, COL_Z, COL_GOUT, COL_GA, COL_XBC, COL_GB = 0, 512, 1024, 2048, 4096, 5120, 6144, 9216
MAIN_W = 10240
SMALL_W = 128
SMALL_GLOW = 0
SMALL_DT = 16

V7X_VMEM_LIMIT = 56 * 1024 * 1024


def _dot(a, b):
    return jnp.dot(a.astype(BF16), b.astype(BF16), preferred_element_type=F32)


def _dot_nt(a, b):
    return lax.dot_general(a.astype(BF16), b.astype(BF16), (((1,), (1,)), ((), ())),
                           preferred_element_type=F32)


def _dot_tn(a, b):
    return lax.dot_general(a.astype(BF16), b.astype(BF16), (((0,), (0,)), ((), ())),
                           preferred_element_type=F32)


def _split3(x):
    x1 = x.astype(BF16)
    r = x - x1.astype(F32)
    x2 = r.astype(BF16)
    x3 = (r - x2.astype(F32)).astype(BF16)
    return x1, x2, x3


def _dot_exact_l(m, x):
    x1, x2, x3 = _split3(x)
    return (jnp.dot(m, x1, preferred_element_type=F32) + jnp.dot(m, x2, preferred_element_type=F32)
            + jnp.dot(m, x3, preferred_element_type=F32))


def _dot_exact_r(x, m):
    x1, x2, x3 = _split3(x)
    return (jnp.dot(x1, m, preferred_element_type=F32) + jnp.dot(x2, m, preferred_element_type=F32)
            + jnp.dot(x3, m, preferred_element_type=F32))


def _sigmoid(x):
    return 1.0 / (1.0 + jnp.exp(-x))


def _silu(x):
    return x * _sigmoid(x)


def _softplus(x):
    return jnp.maximum(x, 0.0) + jnp.log1p(jnp.exp(-jnp.abs(x)))


def _inproj_kernel(x_ref, lnw_ref, w_ref, ws_ref, o_ref, os_ref, hn_ref):
    @pl.when(pl.program_id(1) == 0)
    def _():
        x = x_ref[...]
        ms = jnp.mean(x * x, axis=-1, keepdims=True)
        hn_ref[...] = (x * lax.rsqrt(ms + RMS_EPS) * lnw_ref[...]).astype(BF16)
        os_ref[...] = jnp.dot(hn_ref[...], ws_ref[...], preferred_element_type=F32)

    o_ref[...] = jnp.dot(hn_ref[...], w_ref[...], preferred_element_type=F32)


def _inproj(x2d, lnw, w_main, w_small):
    rows = x2d.shape[0]
    tm = min(1024, rows)
    tn = 1024
    return pl.pallas_call(
        _inproj_kernel,
        grid=(rows // tm, MAIN_W // tn),
        in_specs=[
            pl.BlockSpec((tm, D_MODEL), lambda i, j: (i, 0)),
            pl.BlockSpec((1, D_MODEL), lambda i, j: (0, 0)),
            pl.BlockSpec((D_MODEL, tn), lambda i, j: (0, j)),
            pl.BlockSpec((D_MODEL, SMALL_W), lambda i, j: (0, 0)),
        ],
        out_specs=[
            pl.BlockSpec((tm, tn), lambda i, j: (i, j)),
            pl.BlockSpec((tm, SMALL_W), lambda i, j: (i, 0)),
        ],
        out_shape=[
            jax.ShapeDtypeStruct((rows, MAIN_W), F32),
            jax.ShapeDtypeStruct((rows, SMALL_W), F32),
        ],
        scratch_shapes=[pltpu.VMEM((tm, D_MODEL), BF16)],
        compiler_params=pltpu.CompilerParams(
            dimension_semantics=("arbitrary", "arbitrary"), vmem_limit_bytes=V7X_VMEM_LIMIT),
        name="inproj",
    )(x2d, lnw, w_main, w_small)


def _gla_kernel(q_ref, k_ref, v_ref, g_ref, sm_ref, w2_ref, bg_ref, nw_ref, s0_ref, tri_ref,
                og_ref, sfin_ref, s_ref, *, cb, mask_rows):
    c = pl.program_id(1)

    @pl.when(c == 0)
    def _():
        s_ref[...] = s0_ref[...]

    tri = tri_ref[...]
    tri_b = (lax.broadcasted_iota(jnp.int32, (CHUNK, CHUNK), 1)
             <= lax.broadcasted_iota(jnp.int32, (CHUNK, CHUNK), 0))
    scale = GLA_DK ** -0.5

    def chunk_step(ci, carry):
        rs = pl.ds(pl.multiple_of(ci * CHUNK, CHUNK), CHUNK)
        lg = _dot(sm_ref[rs, :], w2_ref[...]) + bg_ref[...]
        lg = (jnp.minimum(lg, 0.0) - jnp.log1p(jnp.exp(-jnp.abs(lg)))) * (1.0 / GLA_GATE_NORM)
        if mask_rows:
            keep = lax.broadcasted_iota(jnp.int32, (CHUNK, GLA_QK), 0) >= mask_rows
            lg = jnp.where(keep, lg, 0.0)
        gcum = _dot_exact_l(tri, lg)
        for h in range(GLA_HEADS):
            ks = slice(h * GLA_DK, (h + 1) * GLA_DK)
            vs = slice(h * GLA_DV, (h + 1) * GLA_DV)
            qh = q_ref[rs, ks] * scale
            kh = k_ref[rs, ks]
            vh = v_ref[rs, vs]
            if mask_rows:
                keep_k = lax.broadcasted_iota(jnp.int32, (CHUNK, GLA_DK), 0) >= mask_rows
                keep_v = lax.broadcasted_iota(jnp.int32, (CHUNK, GLA_DV), 0) >= mask_rows
                qh = jnp.where(keep_k, qh, 0.0)
                kh = jnp.where(keep_k, kh, 0.0)
                vh = jnp.where(keep_v, vh, 0.0)
            gh = gcum[:, ks]
            eg = jnp.exp(gh)
            egn = jnp.exp(-gh)
            qe = qh * eg
            a_causal = _dot_nt(qe, kh * egn)
            a_ahead = _dot_nt(qh * egn, kh * eg)
            att = jnp.where(tri_b, a_causal, a_ahead)
            st = s_ref[h]
            o = _dot(att, vh) + _dot_nt(qe, st)
            glast = gh[CHUNK - 1:CHUNK, :]
            kd = kh * jnp.exp(glast - gh)
            s_ref[h] = st * jnp.exp(glast) + _dot_tn(vh, kd)
            ms = jnp.mean(o * o, axis=-1, keepdims=True)
            on = o * lax.rsqrt(ms + RMS_EPS) * nw_ref[...]
            og_ref[rs, vs] = (on * _silu(g_ref[rs, vs])).astype(og_ref.dtype)
        return carry

    lax.fori_loop(0, cb, chunk_step, 0)

    @pl.when(c == pl.num_programs(1) - 1)
    def _():
        sfin_ref[0] = s_ref[...]


def _gla(proj, small, w2p, bgate, normw, s0, tri, *, batch, seq, cb, mask_rows):
    nb = seq // (cb * CHUNK)
    rb = cb * CHUNK
    row = lambda b, c: b * nb + c
    kern = functools.partial(_gla_kernel, cb=cb, mask_rows=mask_rows)
    return pl.pallas_call(
        kern,
        grid=(batch, nb),
        in_specs=[
            pl.BlockSpec((rb, GLA_QK), lambda b, c: (row(b, c), COL_Q // GLA_QK)),
            pl.BlockSpec((rb, GLA_QK), lambda b, c: (row(b, c), COL_K // GLA_QK)),
            pl.BlockSpec((rb, GLA_VW), lambda b, c: (row(b, c), COL_V // GLA_VW)),
            pl.BlockSpec((rb, GLA_VW), lambda b, c: (row(b, c), COL_GOUT // GLA_VW)),
            pl.BlockSpec((rb, SMALL_W), lambda b, c: (row(b, c), 0)),
            pl.BlockSpec((SMALL_W, GLA_QK), lambda b, c: (0, 0)),
            pl.BlockSpec((1, GLA_QK), lambda b, c: (0, 0)),
            pl.BlockSpec((1, GLA_DV), lambda b, c: (0, 0)),
            pl.BlockSpec((GLA_HEADS, GLA_DV, GLA_DK), lambda b, c: (0, 0, 0)),
            pl.BlockSpec((CHUNK, CHUNK), lambda b, c: (0, 0)),
        ],
        out_specs=[
            pl.BlockSpec((rb, GLA_VW), lambda b, c: (row(b, c), 0)),
            pl.BlockSpec((1, GLA_HEADS, GLA_DV, GLA_DK), lambda b, c: (b, 0, 0, 0)),
        ],
        out_shape=[
            jax.ShapeDtypeStruct((batch * seq, GLA_VW), BF16),
            jax.ShapeDtypeStruct((batch, GLA_HEADS, GLA_DV, GLA_DK), F32),
        ],
        scratch_shapes=[pltpu.VMEM((GLA_HEADS, GLA_DV, GLA_DK), F32)],
        compiler_params=pltpu.CompilerParams(
            dimension_semantics=("arbitrary", "arbitrary"), vmem_limit_bytes=V7X_VMEM_LIMIT),
        name="gla_scan",
    )(proj, proj, proj, proj, small, w2p, bgate, normw, s0, tri)


def _ssd_kernel(xbc_ref, z_ref, sm_ref, cw_ref, cbias_ref, dtb_ref, alog_ref, dsk_ref, nw_ref,
                s0_ref, u0_ref, tri_ref, ones_ref, exp_ref, dm_ref, bmask_ref,
                yg_ref, sfin_ref, ufin_ref, s_ref, prev_ref, *, cb, mask_rows):
    c = pl.program_id(1)

    @pl.when(c == 0)
    def _():
        s_ref[...] = s0_ref[...]
        prev_ref[...] = u0_ref[...]

    tri = tri_ref[...]
    ones64 = ones_ref[...]
    expand = exp_ref[...]
    row_x = lax.broadcasted_iota(jnp.int32, (CHUNK, SSD_XBC), 0)

    def chunk_step(ci, carry):
        rs = pl.ds(pl.multiple_of(ci * CHUNK, CHUNK), CHUNK)
        u = xbc_ref[rs, :]
        prev = prev_ref[...]
        conv = u * cw_ref[SSD_CONV - 1:SSD_CONV, :] + cbias_ref[...]
        for j in range(1, SSD_CONV):
            shifted = jnp.where(row_x < j, pltpu.roll(prev, j, 0), pltpu.roll(u, j, 0))
            conv = conv + shifted * cw_ref[SSD_CONV - 1 - j:SSD_CONV - j, :]
        prev_ref[...] = u
        xc = _silu(conv)
        dt = _softplus(sm_ref[rs, :] + dtb_ref[...])
        if mask_rows:
            xc = jnp.where(row_x >= mask_rows, xc, 0.0)
            dt = jnp.where(lax.broadcasted_iota(jnp.int32, (CHUNK, SMALL_W), 0) >= mask_rows, dt, 0.0)
        a = -dt * jnp.exp(alog_ref[...])
        xs = xc[:, :SSD_D_INNER]
        bm = xc[:, SSD_D_INNER:SSD_D_INNER + SSD_BC]
        cm = xc[:, SSD_D_INNER + SSD_BC:]
        acum = _dot_exact_l(tri, a)
        colb = _dot_exact_r(acum, expand)
        dtx = _dot_exact_r(dt, expand)
        rowb = _dot_exact_l(ones64, colb * dm_ref[...])
        seg = jnp.exp(-jnp.abs(colb - rowb))
        clast = colb[CHUNK - 1:CHUNK, :]
        eac = jnp.exp(colb)
        toend = jnp.exp(clast - colb)
        dec = jnp.exp(clast)
        xdt = xs * dtx
        for g in range(SSD_GROUPS):
            gs = slice(g * SSD_GW, (g + 1) * SSD_GW)
            ns = slice(g * SSD_STATE, (g + 1) * SSD_STATE)
            bg = bm[:, ns]
            cg = cm[:, ns]
            xg = xdt[:, gs]
            cbx = _dot_nt(cg, jnp.concatenate([bg] * SSD_HPG, axis=0))
            sc = seg[:, gs] * cbx
            bd = jnp.concatenate([xg.astype(BF16)] * SSD_HPG, axis=0) * bmask_ref[...]
            st = s_ref[g]
            y = _dot(sc, bd) + _dot(cg, st) * eac[:, gs]
            s_ref[g] = st * dec[:, gs] + _dot_tn(bg, xg * toend[:, gs])
            y = y + dsk_ref[:, gs] * xs[:, gs]
            y = y * _silu(z_ref[rs, gs])
            ms = jnp.mean(y * y, axis=-1, keepdims=True)
            yg_ref[rs, gs] = (y * lax.rsqrt(ms + RMS_EPS) * nw_ref[:, gs]).astype(yg_ref.dtype)
        return carry

    lax.fori_loop(0, cb, chunk_step, 0)

    @pl.when(c == pl.num_programs(1) - 1)
    def _():
        sfin_ref[0] = s_ref[...]
        ufin_ref[0] = prev_ref[...]


def _ssd(proj, small, cw, cbias, dtb, alog, dsk, normw, s0, u0, consts, *, batch, seq, cb, mask_rows):
    nb = seq // (cb * CHUNK)
    rb = cb * CHUNK
    row = lambda b, c: b * nb + c
    tri, ones64, expand, dmask, bmask = consts
    kern = functools.partial(_ssd_kernel, cb=cb, mask_rows=mask_rows)
    full2 = lambda shape: pl.BlockSpec(shape, lambda b, c: (0, 0))
    return pl.pallas_call(
        kern,
        grid=(batch, nb),
        in_specs=[
            pl.BlockSpec((rb, SSD_XBC), lambda b, c: (row(b, c), COL_XBC // SSD_XBC)),
            pl.BlockSpec((rb, SSD_D_INNER), lambda b, c: (row(b, c), COL_Z // SSD_D_INNER)),
            pl.BlockSpec((rb, SMALL_W), lambda b, c: (row(b, c), 0)),
            full2((SSD_CONV, SSD_XBC)),
            full2((1, SSD_XBC)),
            full2((1, SMALL_W)),
            full2((1, SMALL_W)),
            full2((1, SSD_D_INNER)),
            full2((1, SSD_D_INNER)),
            pl.BlockSpec((SSD_GROUPS, SSD_STATE, SSD_GW), lambda b, c: (0, 0, 0)),
            full2((CHUNK, SSD_XBC)),
            full2((CHUNK, CHUNK)),
            full2((CHUNK, CHUNK)),
            full2((SMALL_W, SSD_D_INNER)),
            full2((CHUNK, SSD_D_INNER)),
            full2((SSD_GW, SSD_GW)),
        ],
        out_specs=[
            pl.BlockSpec((rb, SSD_D_INNER), lambda b, c: (row(b, c), 0)),
            pl.BlockSpec((1, SSD_GROUPS, SSD_STATE, SSD_GW), lambda b, c: (b, 0, 0, 0)),
            pl.BlockSpec((1, CHUNK, SSD_XBC), lambda b, c: (b, 0, 0)),
        ],
        out_shape=[
            jax.ShapeDtypeStruct((batch * seq, SSD_D_INNER), BF16),
            jax.ShapeDtypeStruct((batch, SSD_GROUPS, SSD_STATE, SSD_GW), F32),
            jax.ShapeDtypeStruct((batch, CHUNK, SSD_XBC), F32),
        ],
        scratch_shapes=[pltpu.VMEM((SSD_GROUPS, SSD_STATE, SSD_GW), F32),
                        pltpu.VMEM((CHUNK, SSD_XBC), F32)],
        compiler_params=pltpu.CompilerParams(
            dimension_semantics=("arbitrary", "arbitrary"), vmem_limit_bytes=V7X_VMEM_LIMIT),
        name="ssd_scan",
    )(proj, proj, small, cw, cbias, dtb, alog, dsk, normw, s0, u0, tri, ones64, expand, dmask, bmask)


def _merge_kernel(og_ref, yg_ref, ga_ref, gb_ref, x_ref, wa_ref, wb_ref, wo_ref, lnw_ref, h_ref, hn_ref):
    ya = jnp.dot(og_ref[...], wa_ref[...], preferred_element_type=F32)
    yb = jnp.dot(yg_ref[...], wb_ref[...], preferred_element_type=F32)
    mixed = _sigmoid(ga_ref[...]) * ya + _sigmoid(gb_ref[...]) * yb
    h = x_ref[...] + _dot(mixed, wo_ref[...])
    h_ref[...] = h
    ms = jnp.mean(h * h, axis=-1, keepdims=True)
    hn_ref[...] = (h * lax.rsqrt(ms + RMS_EPS) * lnw_ref[...]).astype(BF16)


def _merge(og, yg, proj, x2d, wa, wb, wo, lnw):
    rows = x2d.shape[0]
    tm = min(512, rows)
    full = lambda shape: pl.BlockSpec(shape, lambda i: (0, 0))
    return pl.pallas_call(
        _merge_kernel,
        grid=(rows // tm,),
        in_specs=[
            pl.BlockSpec((tm, GLA_VW), lambda i: (i, 0)),
            pl.BlockSpec((tm, SSD_D_INNER), lambda i: (i, 0)),
            pl.BlockSpec((tm, D_MODEL), lambda i: (i, COL_GA // D_MODEL)),
            pl.BlockSpec((tm, D_MODEL), lambda i: (i, COL_GB // D_MODEL)),
            pl.BlockSpec((tm, D_MODEL), lambda i: (i, 0)),
            full((GLA_VW, D_MODEL)),
            full((SSD_D_INNER, D_MODEL)),
            full((D_MODEL, D_MODEL)),
            full((1, D_MODEL)),
        ],
        out_specs=[
            pl.BlockSpec((tm, D_MODEL), lambda i: (i, 0)),
            pl.BlockSpec((tm, D_MODEL), lambda i: (i, 0)),
        ],
        out_shape=[
            jax.ShapeDtypeStruct((rows, D_MODEL), F32),
            jax.ShapeDtypeStruct((rows, D_MODEL), BF16),
        ],
        compiler_params=pltpu.CompilerParams(
            dimension_semantics=("arbitrary",), vmem_limit_bytes=V7X_VMEM_LIMIT),
        name="merge",
    )(og, yg, proj, proj, x2d, wa, wb, wo, lnw)


PEER_LB = 512
PEER_CW = 128


def _topk_rows(vals, codes, k, out_v_ref, out_i_ref, row0, ls):
    for r in range(k):
        m = jnp.max(vals, axis=0, keepdims=True)
        idx = jnp.min(jnp.where(vals == m, codes, _CODE_NONE), axis=0, keepdims=True)
        out_v_ref[row0 + r:row0 + r + 1, ls] = m
        out_i_ref[row0 + r:row0 + r + 1, ls] = idx
        vals = jnp.where(codes == idx, -jnp.inf, vals)


_CODE_NONE = float(1 << 20)
PEER_NB = (16, 8, 5, 4, 3, 2, 2, 2)
PEER_CAND_ROWS = 16 + 8 * 7 + 8


def _cand_codes(lanes):
    r = lax.broadcasted_iota(jnp.int32, (PEER_CAND_ROWS, lanes), 0)
    mid = r - 16
    code_mid = (((mid >> 3) + 1) << 4) + (mid & 7)
    code_last = (r - 72 + 8) << 4
    return jnp.where(r < 16, r, jnp.where(r < 72, code_mid, code_last)).astype(F32)


def _select_kernel(hn_ref, wq_ref, keys_ref, i_ref, j_ref, w_ref,
                   tv_ref, ti_ref, cand_ref, bv_ref, bp_ref, oi_ref, oj_ref, ow_ref):
    h = pl.program_id(1)
    hn = hn_ref[...]
    scores = []
    for half in range(2):
        qt = lax.dot_general(wq_ref[half * PEER_HALF:(half + 1) * PEER_HALF, :], hn,
                             (((1,), (1,)), ((), ())), preferred_element_type=F32)
        scores.append(jnp.dot(keys_ref[0, half], qt.astype(BF16), preferred_element_type=F32))
    key_codes = lax.broadcasted_iota(jnp.int32, (PEER_N_KEYS, PEER_CW), 0).astype(F32)
    cand_codes = _cand_codes(PEER_CW)
    sub8 = lax.broadcasted_iota(jnp.int32, (8, PEER_CW), 0)
    for lb in range(hn.shape[0] // PEER_CW):
        ls = slice(lb * PEER_CW, (lb + 1) * PEER_CW)
        for half in range(2):
            _topk_rows(scores[half][:, ls], key_codes, PEER_TOPK, tv_ref, ti_ref, half * PEER_TOPK, ls)
        tv1 = tv_ref[PEER_TOPK:2 * PEER_TOPK, ls]
        cand_ref[0:PEER_TOPK, ls] = tv_ref[0:1, ls] + tv1
        for a in range(1, 8):
            row = 16 + 8 * (a - 1)
            cand_ref[row:row + 8, ls] = jnp.where(sub8 < PEER_NB[a], tv_ref[a:a + 1, ls] + tv1[0:8, :], -jnp.inf)
        cand_ref[72:80, ls] = tv_ref[8:PEER_TOPK, ls] + tv1[0:1, :]
        _topk_rows(cand_ref[:, ls], cand_codes, PEER_TOPK, bv_ref, bp_ref, 0, ls)
    best = bv_ref[...]
    pos = bp_ref[...].astype(jnp.int32)
    a_sel = pos >> 4
    b_sel = pos & (PEER_TOPK - 1)
    isel = jnp.zeros_like(best)
    jsel = jnp.zeros_like(best)
    for a in range(PEER_TOPK):
        isel = jnp.where(a_sel == a, ti_ref[a:a + 1, :], isel)
        jsel = jnp.where(b_sel == a, ti_ref[PEER_TOPK + a:PEER_TOPK + a + 1, :], jsel)
    e = jnp.exp(best - best[0:1, :])
    wgt = e / jnp.sum(e, axis=0, keepdims=True)
    hs = pl.ds(pl.multiple_of(h * PEER_TOPK, PEER_TOPK), PEER_TOPK)
    oi_ref[hs, :] = isel
    oj_ref[hs, :] = jsel
    ow_ref[hs, :] = wgt

    @pl.when(h == PEER_HEADS - 1)
    def _():
        i_ref[...] = oi_ref[...].T
        j_ref[...] = oj_ref[...].T
        w_ref[...] = ow_ref[...].T


def _select(hn2, wq_t, keys):
    t = hn2.shape[0]
    tn = PEER_LB
    out = jax.ShapeDtypeStruct((t, PEER_SLOTS), F32)
    return pl.pallas_call(
        _select_kernel,
        grid=(t // tn, PEER_HEADS),
        in_specs=[
            pl.BlockSpec((tn, D_MODEL), lambda i, h: (i, 0)),
            pl.BlockSpec((2 * PEER_HALF, D_MODEL), lambda i, h: (h, 0)),
            pl.BlockSpec((1, 2, PEER_N_KEYS, PEER_HALF), lambda i, h: (h, 0, 0, 0)),
        ],
        out_specs=[pl.BlockSpec((tn, PEER_SLOTS), lambda i, h: (i, 0))] * 3,
        out_shape=[out, out, out],
        scratch_shapes=[
            pltpu.VMEM((2 * PEER_TOPK, tn), F32),
            pltpu.VMEM((2 * PEER_TOPK, tn), F32),
            pltpu.VMEM((PEER_CAND_ROWS, tn), F32),
            pltpu.VMEM((PEER_TOPK, tn), F32),
            pltpu.VMEM((PEER_TOPK, tn), F32),
            pltpu.VMEM((PEER_SLOTS, tn), F32),
            pltpu.VMEM((PEER_SLOTS, tn), F32),
            pltpu.VMEM((PEER_SLOTS, tn), F32),
        ],
        compiler_params=pltpu.CompilerParams(
            dimension_semantics=("arbitrary", "arbitrary"), vmem_limit_bytes=V7X_VMEM_LIMIT),
        name="peer_select",
    )(hn2, wq_t, keys)


PEER_TN = 512
PEER_EB = 1024
PEER_IH = 64
PEER_PITCH = PEER_TN + 8


def _gelu_tanh(x):
    return 0.5 * x * (1.0 + jnp.tanh(math.sqrt(2.0 / math.pi) * (x + 0.044715 * (x * x * x))))


def _expert_kernel(hn_ref, i_ref, j_ref, w_ref, h_ref, u_ref, v_ref, lnw_ref, o_ref, acc_ref, r_ref,
                   *, tn, pitch):
    e = pl.program_id(1)
    steps_per_half = (PEER_IH * PEER_N_KEYS) // PEER_EB
    ipb = PEER_EB // PEER_N_KEYS

    @pl.when(e == 0)
    def _():
        acc_ref[...] = jnp.zeros_like(acc_ref)

    @pl.when(e % steps_per_half == 0)
    def _():
        i_base = (e // steps_per_half) * PEER_IH
        sub = (lax.broadcasted_iota(jnp.int32, (PEER_IH, PEER_SLOTS), 0) + i_base).astype(F32)
        subj = lax.broadcasted_iota(jnp.int32, (PEER_N_KEYS, PEER_SLOTS), 0).astype(F32)

        def body(n, carry):
            irow = i_ref[pl.ds(n, 1), :]
            jrow = j_ref[pl.ds(n, 1), :]
            wrow = w_ref[pl.ds(n, 1), :]
            w_hi = wrow.astype(BF16)
            w_lo = (wrow - w_hi.astype(F32)).astype(BF16)
            hit = sub == irow
            q_hi = jnp.where(hit, w_hi.astype(F32), 0.0).astype(BF16)
            q_lo = jnp.where(hit, w_lo.astype(F32), 0.0).astype(BF16)
            p = jnp.where(subj == jrow, 1.0, 0.0).astype(BF16)
            r = (lax.dot_general(q_hi, p, (((1,), (1,)), ((), ())), preferred_element_type=F32)
                 + lax.dot_general(q_lo, p, (((1,), (1,)), ((), ())), preferred_element_type=F32))
            r_ref[pl.ds(n, PEER_IH, stride=pitch), :] = r
            return carry

        lax.fori_loop(0, tn, body, 0, unroll=32)

    i_loc = (e % steps_per_half) * ipb
    act = lax.dot_general(hn_ref[...], u_ref[...], (((1,), (1,)), ((), ())),
                          preferred_element_type=F32)
    wblk = jnp.concatenate(
        [r_ref[pl.ds(pl.multiple_of((i_loc + ii) * pitch, 8), tn), :] for ii in range(ipb)], axis=1)
    g = (_gelu_tanh(act) * wblk).astype(BF16)
    acc_ref[...] += jnp.dot(g, v_ref[...], preferred_element_type=F32)

    @pl.when(e == pl.num_programs(1) - 1)
    def _():
        h = h_ref[...] + acc_ref[...]
        ms = jnp.mean(h * h, axis=-1, keepdims=True)
        o_ref[...] = h * lax.rsqrt(ms + RMS_EPS) * lnw_ref[...]


def _experts(hn2, isel, jsel, wsel, h2, u_tab, v_tab, lnw):
    t = hn2.shape[0]
    tn = min(PEER_TN, t)
    pitch = tn + 8
    n_exp = u_tab.shape[0]
    kern = functools.partial(_expert_kernel, tn=tn, pitch=pitch)
    return pl.pallas_call(
        kern,
        grid=(t // tn, n_exp // PEER_EB),
        in_specs=[
            pl.BlockSpec((tn, D_MODEL), lambda i, e: (i, 0)),
            pl.BlockSpec((tn, PEER_SLOTS), lambda i, e: (i, 0)),
            pl.BlockSpec((tn, PEER_SLOTS), lambda i, e: (i, 0)),
            pl.BlockSpec((tn, PEER_SLOTS), lambda i, e: (i, 0)),
            pl.BlockSpec((tn, D_MODEL), lambda i, e: (i, 0)),
            pl.BlockSpec((PEER_EB, D_MODEL), lambda i, e: (e, 0)),
            pl.BlockSpec((PEER_EB, D_MODEL), lambda i, e: (e, 0)),
            pl.BlockSpec((1, D_MODEL), lambda i, e: (0, 0)),
        ],
        out_specs=pl.BlockSpec((tn, D_MODEL), lambda i, e: (i, 0)),
        out_shape=jax.ShapeDtypeStruct((t, D_MODEL), F32),
        scratch_shapes=[pltpu.VMEM((tn, D_MODEL), F32),
                        pltpu.VMEM((PEER_IH * pitch, PEER_N_KEYS), F32)],
        compiler_params=pltpu.CompilerParams(
            dimension_semantics=("arbitrary", "arbitrary"), vmem_limit_bytes=V7X_VMEM_LIMIT),
        name="peer_experts",
    )(hn2, isel, jsel, wsel, h2, u_tab, v_tab, lnw)


def _constants():
    t = jnp.arange(CHUNK)
    tri = (t[None, :] <= t[:, None]).astype(BF16)
    ones64 = jnp.ones((CHUNK, CHUNK), BF16)
    lane = jnp.arange(SSD_D_INNER)
    expand = (jnp.arange(SMALL_W)[:, None] == SMALL_DT + lane[None, :] // SSD_HEADDIM).astype(BF16)
    dmask = (t[:, None] == lane[None, :] % CHUNK).astype(F32)
    gl = jnp.arange(SSD_GW)
    bmask = (gl[:, None] // CHUNK == gl[None, :] // SSD_HEADDIM).astype(BF16)
    return tri, ones64, expand, dmask, bmask


def _pack_in_weights(w_in):
    o = 0
    parts = {}
    for name, width in (("q", GLA_QK), ("k", GLA_QK), ("v", GLA_VW), ("gout", GLA_VW), ("glow", GLA_GATE_RANK),
                        ("z", SSD_D_INNER), ("xbc", SSD_XBC), ("dt", SSD_HEADS), ("ga", D_MODEL), ("gb", D_MODEL)):
        parts[name] = w_in[:, o:o + width]
        o += width
    main = jnp.concatenate([parts[n] for n in ("q", "k", "v", "z", "gout", "ga", "xbc", "gb")], axis=1)
    small = jnp.concatenate(
        [parts["glow"], parts["dt"], jnp.zeros((D_MODEL, SMALL_W - GLA_GATE_RANK - SSD_HEADS), w_in.dtype)], axis=1)
    return main.astype(BF16), small.astype(BF16)


def _pad_small(vec, offset):
    out = jnp.zeros((1, SMALL_W), F32)
    return out.at[0, offset:offset + vec.shape[0]].set(vec.astype(F32))


def kernel(x, meta_tokens, ln_mix_w, w_in, gla_w_gate2, gla_b_gate, gla_norm_w, ssd_conv_w, ssd_conv_b,
           ssd_dt_bias, ssd_a_log, ssd_d, ssd_norm_w, w_up_gla, w_up_ssd, w_out, ln_ffn_w,
           peer_w_q, peer_sub_keys, peer_u, peer_v, ln_final_w):
    bsz, seq, _ = x.shape
    assert seq % (4 * CHUNK) == 0
    l = 0
    consts = _constants()
    tri = consts[0]

    w_main, w_small = _pack_in_weights(w_in[l])
    lnw = ln_mix_w[l].reshape(1, D_MODEL)
    w2p = jnp.zeros((SMALL_W, GLA_QK), F32).at[SMALL_GLOW:SMALL_GLOW + GLA_GATE_RANK].set(gla_w_gate2[l]).astype(BF16)
    bgate = gla_b_gate[l].reshape(1, GLA_QK)
    gnw = gla_norm_w[l].reshape(1, GLA_DV)
    cw = ssd_conv_w[l]
    cbias = ssd_conv_b[l].reshape(1, SSD_XBC)
    dtb = _pad_small(ssd_dt_bias[l], SMALL_DT)
    alog = _pad_small(ssd_a_log[l], SMALL_DT)
    dsk = jnp.repeat(ssd_d[l].astype(F32), SSD_HEADDIM).reshape(1, SSD_D_INNER)
    snw = ssd_norm_w[l].reshape(1, SSD_D_INNER)

    meta_rows = jnp.concatenate([jnp.zeros((META_PAD, D_MODEL), F32), meta_tokens.astype(F32)], axis=0)
    proj_m, small_m = _inproj(meta_rows, lnw, w_main, w_small)
    gla_s0 = jnp.zeros((GLA_HEADS, GLA_DV, GLA_DK), F32)
    ssd_s0 = jnp.zeros((SSD_GROUPS, SSD_STATE, SSD_GW), F32)
    u0 = jnp.zeros((CHUNK, SSD_XBC), F32)
    _, gla_s1 = _gla(proj_m, small_m, w2p, bgate, gnw, gla_s0, tri, batch=1, seq=CHUNK, cb=1, mask_rows=META_PAD)
    _, ssd_s1, u1 = _ssd(proj_m, small_m, cw, cbias, dtb, alog, dsk, snw, ssd_s0, u0, consts,
                         batch=1, seq=CHUNK, cb=1, mask_rows=META_PAD)

    x2d = x.reshape(bsz * seq, D_MODEL)
    proj, small = _inproj(x2d, lnw, w_main, w_small)
    og, _ = _gla(proj, small, w2p, bgate, gnw, gla_s1[0], tri, batch=bsz, seq=seq, cb=4, mask_rows=0)
    yg, _, _ = _ssd(proj, small, cw, cbias, dtb, alog, dsk, snw, ssd_s1[0], u1[0], consts,
                    batch=bsz, seq=seq, cb=4, mask_rows=0)
    h2, hn2 = _merge(og, yg, proj, x2d, w_up_gla[l].astype(BF16), w_up_ssd[l].astype(BF16),
                     w_out[l].astype(BF16), ln_ffn_w[l].reshape(1, D_MODEL))

    wq_t = peer_w_q[l].T.astype(BF16)
    keys = peer_sub_keys[l].astype(BF16)
    isel, jsel, wsel = _select(hn2, wq_t, keys)
    out = _experts(hn2, isel, jsel, wsel, h2, peer_u[l].astype(BF16), peer_v[l].astype(BF16),
                   ln_final_w.reshape(1, D_MODEL))
    return out.reshape(bsz, seq, D_MODEL)
```

```python
import functools
import math

import jax
import jax.numpy as jnp
from jax import lax
from jax.experimental import pallas as pl
from jax.experimental.pallas import tpu as pltpu

F32 = jnp.float32
BF16 = jnp.bfloat16

D_MODEL = 1024
CHUNK = 64
N_META = 16
META_PAD = CHUNK - N_META
RMS_EPS = 1e-6
GLA_HEADS = 4
GLA_DK = 128
GLA_DV = 256
GLA_GATE_RANK = 16
GLA_GATE_NORM = 16.0
GLA_QK = 512
GLA_VW = 1024
SSD_D_INNER = 2048
SSD_HEADDIM = 64
SSD_HEADS = 32
SSD_GROUPS = 4
SSD_HPG = 8
SSD_STATE = 128
SSD_CONV = 4
SSD_BC = 512
SSD_XBC = 3072
SSD_GW = SSD_D_INNER // SSD_GROUPS
PEER_HEADS = 8
PEER_N_KEYS = 128
PEER_HALF = 128
PEER_TOPK = 16
PEER_SLOTS = PEER_HEADS * PEER_TOPK

COL_Q, COL_K, COL_V, COL_Z, COL_GOUT, COL_GA, COL_XBC, COL_GB = 0, 512, 1024, 2048, 4096, 5120, 6144, 9216
MAIN_W = 10240
SMALL_W = 128
SMALL_GLOW = 0
SMALL_DT = 16

V7X_VMEM_LIMIT = 56 * 1024 * 1024
V7X_VMEM_LIMIT_EXPERTS = 60 * 1024 * 1024


def _dot(a, b):
    return jnp.dot(a.astype(BF16), b.astype(BF16), preferred_element_type=F32)


def _dot_nt(a, b):
    return lax.dot_general(a.astype(BF16), b.astype(BF16), (((1,), (1,)), ((), ())),
                           preferred_element_type=F32)


def _dot_tn(a, b):
    return lax.dot_general(a.astype(BF16), b.astype(BF16), (((0,), (0,)), ((), ())),
                           preferred_element_type=F32)


def _split3(x):
    x1 = x.astype(BF16)
    r = x - x1.astype(F32)
    x2 = r.astype(BF16)
    x3 = (r - x2.astype(F32)).astype(BF16)
    return x1, x2, x3


def _dot_exact_l(m, x):
    x1, x2, x3 = _split3(x)
    return (jnp.dot(m, x1, preferred_element_type=F32) + jnp.dot(m, x2, preferred_element_type=F32)
            + jnp.dot(m, x3, preferred_element_type=F32))


def _sigmoid(x):
    return 0.5 * jnp.tanh(0.5 * x) + 0.5


def _silu(x):
    return x * _sigmoid(x)


def _softplus(x):
    return jnp.maximum(x, 0.0) + jnp.log1p(jnp.exp(-jnp.abs(x)))


def _inproj_kernel(x_ref, lnw_ref, w_ref, ws_ref, o_ref, os_ref, hn_ref):
    @pl.when(pl.program_id(1) == 0)
    def _():
        x = x_ref[...]
        ms = jnp.mean(x * x, axis=-1, keepdims=True)
        hn_ref[...] = (x * lax.rsqrt(ms + RMS_EPS) * lnw_ref[...]).astype(BF16)
        os_ref[...] = jnp.dot(hn_ref[...], ws_ref[...], preferred_element_type=F32)

    o_ref[...] = jnp.dot(hn_ref[...], w_ref[...], preferred_element_type=F32)


def _inproj(x2d, lnw, w_main, w_small):
    rows = x2d.shape[0]
    tm = min(1024, rows)
    tn = 1024
    return pl.pallas_call(
        _inproj_kernel,
        grid=(rows // tm, MAIN_W // tn),
        in_specs=[
            pl.BlockSpec((tm, D_MODEL), lambda i, j: (i, 0)),
            pl.BlockSpec((1, D_MODEL), lambda i, j: (0, 0)),
            pl.BlockSpec((D_MODEL, tn), lambda i, j: (0, j)),
            pl.BlockSpec((D_MODEL, SMALL_W), lambda i, j: (0, 0)),
        ],
        out_specs=[
            pl.BlockSpec((tm, tn), lambda i, j: (i, j)),
            pl.BlockSpec((tm, SMALL_W), lambda i, j: (i, 0)),
        ],
        out_shape=[
            jax.ShapeDtypeStruct((rows, MAIN_W), F32),
            jax.ShapeDtypeStruct((rows, SMALL_W), F32),
        ],
        scratch_shapes=[pltpu.VMEM((tm, D_MODEL), BF16)],
        compiler_params=pltpu.CompilerParams(
            dimension_semantics=("arbitrary", "arbitrary"), vmem_limit_bytes=V7X_VMEM_LIMIT),
        name="inproj",
    )(x2d, lnw, w_main, w_small)


def _gla_kernel(q_ref, k_ref, v_ref, g_ref, sm_ref, w2_ref, bg_ref, nw_ref, s0_ref, tri_ref,
                og_ref, sfin_ref, s_ref, *, cb, mask_rows):
    c = pl.program_id(1)

    @pl.when(c == 0)
    def _():
        s_ref[...] = s0_ref[...]

    tri = tri_ref[...]
    tri_b = (lax.broadcasted_iota(jnp.int32, (CHUNK, CHUNK), 1)
             <= lax.broadcasted_iota(jnp.int32, (CHUNK, CHUNK), 0))
    scale = GLA_DK ** -0.5

    def chunk_step(ci, carry):
        rs = pl.ds(pl.multiple_of(ci * CHUNK, CHUNK), CHUNK)
        lg = _dot(sm_ref[rs, :], w2_ref[...]) + bg_ref[...]
        lg = (jnp.minimum(lg, 0.0) - jnp.log1p(jnp.exp(-jnp.abs(lg)))) * (1.0 / GLA_GATE_NORM)
        if mask_rows:
            keep = lax.broadcasted_iota(jnp.int32, (CHUNK, GLA_QK), 0) >= mask_rows
            lg = jnp.where(keep, lg, 0.0)
        gcum = _dot_exact_l(tri, lg)
        for h in range(GLA_HEADS):
            ks = slice(h * GLA_DK, (h + 1) * GLA_DK)
            vs = slice(h * GLA_DV, (h + 1) * GLA_DV)
            qh = q_ref[rs, ks] * scale
            kh = k_ref[rs, ks]
            vh = v_ref[rs, vs]
            if mask_rows:
                keep_k = lax.broadcasted_iota(jnp.int32, (CHUNK, GLA_DK), 0) >= mask_rows
                keep_v = lax.broadcasted_iota(jnp.int32, (CHUNK, GLA_DV), 0) >= mask_rows
                qh = jnp.where(keep_k, qh, 0.0)
                kh = jnp.where(keep_k, kh, 0.0)
                vh = jnp.where(keep_v, vh, 0.0)
            gh = gcum[:, ks]
            eg = jnp.exp(gh)
            egn = jnp.exp(-gh)
            qe = qh * eg
            a_causal = _dot_nt(qe, kh * egn)
            a_ahead = _dot_nt(qh * egn, kh * eg)
            att = jnp.where(tri_b, a_causal, a_ahead)
            st = s_ref[h]
            o = _dot(att, vh) + _dot_nt(qe, st)
            glast = gh[CHUNK - 1:CHUNK, :]
            kd = kh * jnp.exp(glast - gh)
            s_ref[h] = st * jnp.exp(glast) + _dot_tn(vh, kd)
            ms = jnp.mean(o * o, axis=-1, keepdims=True)
            on = o * lax.rsqrt(ms + RMS_EPS) * nw_ref[...]
            og_ref[rs, vs] = (on * _silu(g_ref[rs, vs])).astype(og_ref.dtype)
        return carry

    lax.fori_loop(0, cb, chunk_step, 0)

    @pl.when(c == pl.num_programs(1) - 1)
    def _():
        sfin_ref[0] = s_ref[...]


def _gla(proj, small, w2p, bgate, normw, s0, tri, *, batch, seq, cb, mask_rows):
    nb = seq // (cb * CHUNK)
    rb = cb * CHUNK
    row = lambda b, c: b * nb + c
    kern = functools.partial(_gla_kernel, cb=cb, mask_rows=mask_rows)
    return pl.pallas_call(
        kern,
        grid=(batch, nb),
        in_specs=[
            pl.BlockSpec((rb, GLA_QK), lambda b, c: (row(b, c), COL_Q // GLA_QK)),
            pl.BlockSpec((rb, GLA_QK), lambda b, c: (row(b, c), COL_K // GLA_QK)),
            pl.BlockSpec((rb, GLA_VW), lambda b, c: (row(b, c), COL_V // GLA_VW)),
            pl.BlockSpec((rb, GLA_VW), lambda b, c: (row(b, c), COL_GOUT // GLA_VW)),
            pl.BlockSpec((rb, SMALL_W), lambda b, c: (row(b, c), 0)),
            pl.BlockSpec((SMALL_W, GLA_QK), lambda b, c: (0, 0)),
            pl.BlockSpec((1, GLA_QK), lambda b, c: (0, 0)),
            pl.BlockSpec((1, GLA_DV), lambda b, c: (0, 0)),
            pl.BlockSpec((GLA_HEADS, GLA_DV, GLA_DK), lambda b, c: (0, 0, 0)),
            pl.BlockSpec((CHUNK, CHUNK), lambda b, c: (0, 0)),
        ],
        out_specs=[
            pl.BlockSpec((rb, GLA_VW), lambda b, c: (row(b, c), 0)),
            pl.BlockSpec((1, GLA_HEADS, GLA_DV, GLA_DK), lambda b, c: (b, 0, 0, 0)),
        ],
        out_shape=[
            jax.ShapeDtypeStruct((batch * seq, GLA_VW), BF16),
            jax.ShapeDtypeStruct((batch, GLA_HEADS, GLA_DV, GLA_DK), F32),
        ],
        scratch_shapes=[pltpu.VMEM((GLA_HEADS, GLA_DV, GLA_DK), F32)],
        compiler_params=pltpu.CompilerParams(
            dimension_semantics=("arbitrary", "arbitrary"), vmem_limit_bytes=V7X_VMEM_LIMIT),
        name="gla_scan",
    )(proj, proj, proj, proj, small, w2p, bgate, normw, s0, tri)


def _ssd_kernel(xbc_ref, z_ref, sm_ref, cw_ref, cbias_ref, dtb_ref, alog_ref, dsk_ref, nw_ref,
                s0_ref, u0_ref, tri_ref, exp_ref, psel_ref, par_ref, bmask_ref,
                yg_ref, sfin_ref, ufin_ref, s_ref, ext_ref, xc_ref, *, cb, mask_rows):
    c = pl.program_id(1)

    @pl.when(c == 0)
    def _():
        s_ref[...] = s0_ref[...]
        ext_ref[0:8, :] = u0_ref[...]

    tri = tri_ref[...]
    psel = psel_ref[...]
    par0 = par_ref[0:1, :]
    par1 = par_ref[1:2, :]
    lane_blk = 512

    def chunk_step(ci, carry):
        rs = pl.ds(pl.multiple_of(ci * CHUNK, CHUNK), CHUNK)
        for lb in range(SSD_XBC // lane_blk):
            ls = slice(lb * lane_blk, (lb + 1) * lane_blk)
            ext_ref[8:8 + CHUNK, ls] = xbc_ref[rs, ls]
            conv = cbias_ref[:, ls] + ext_ref[8:8 + CHUNK, ls] * cw_ref[SSD_CONV - 1:SSD_CONV, ls]
            for j in range(1, SSD_CONV):
                conv = conv + ext_ref[8 - j:8 - j + CHUNK, ls] * cw_ref[SSD_CONV - 1 - j:SSD_CONV - j, ls]
            xc = _silu(conv)
            if mask_rows:
                xc = jnp.where(lax.broadcasted_iota(jnp.int32, (CHUNK, lane_blk), 0) >= mask_rows, xc, 0.0)
            xc_ref[:, ls] = xc
            ext_ref[0:8, ls] = ext_ref[CHUNK:CHUNK + 8, ls]
        dt = _softplus(sm_ref[rs, :] + dtb_ref[...])
        if mask_rows:
            dt = jnp.where(lax.broadcasted_iota(jnp.int32, (CHUNK, SMALL_W), 0) >= mask_rows, dt, 0.0)
        a = -dt * jnp.exp(alog_ref[...])
        acum = _dot_exact_l(tri, a)
        acum_parts = _split3(acum)
        parts6 = jnp.concatenate(acum_parts + _split3(dt), axis=0)
        acum_t = sum(
            lax.dot_general(psel, jnp.concatenate([x * par0, x * par1], axis=0), (((1,), (1,)), ((), ())),
                            preferred_element_type=F32) for x in acum_parts)
        for g in range(SSD_GROUPS):
            gs = slice(g * SSD_GW, (g + 1) * SSD_GW)
            expand = exp_ref[:, gs]
            ex = jnp.dot(parts6, expand, preferred_element_type=F32)
            colb = ex[0:CHUNK] + ex[CHUNK:2 * CHUNK] + ex[2 * CHUNK:3 * CHUNK]
            dtx = ex[3 * CHUNK:4 * CHUNK] + ex[4 * CHUNK:5 * CHUNK] + ex[5 * CHUNK:]
            rowb = jnp.concatenate(
                [jnp.broadcast_to(acum_t[4 * g + k:4 * g + k + 1, :], (CHUNK, 2 * CHUNK)) for k in range(4)],
                axis=1)
            seg = jnp.exp(-jnp.abs(colb - rowb))
            clast = colb[CHUNK - 1:CHUNK, :]
            xs = xc_ref[:, gs]
            bg = xc_ref[:, SSD_D_INNER + g * SSD_STATE:SSD_D_INNER + (g + 1) * SSD_STATE]
            cg = xc_ref[:, SSD_D_INNER + SSD_BC + g * SSD_STATE:SSD_D_INNER + SSD_BC + (g + 1) * SSD_STATE]
            xg = xs * dtx
            cbx = _dot_nt(cg, jnp.concatenate([bg] * SSD_HPG, axis=0))
            sc = seg * cbx
            bd = jnp.concatenate([xg.astype(BF16)] * SSD_HPG, axis=0) * bmask_ref[...]
            st = s_ref[g]
            y = _dot(sc, bd) + _dot(cg, st) * jnp.exp(colb)
            s_ref[g] = st * jnp.exp(clast) + _dot_tn(bg, xg * jnp.exp(clast - colb))
            y = y + dsk_ref[:, gs] * xs
            y = y * _silu(z_ref[rs, gs])
            ms = jnp.mean(y * y, axis=-1, keepdims=True)
            yg_ref[rs, gs] = (y * lax.rsqrt(ms + RMS_EPS) * nw_ref[:, gs]).astype(yg_ref.dtype)
        return carry

    lax.fori_loop(0, cb, chunk_step, 0)

    @pl.when(c == pl.num_programs(1) - 1)
    def _():
        sfin_ref[0] = s_ref[...]
        ufin_ref[0] = ext_ref[0:8, :]


def _ssd(proj, small, cw, cbias, dtb, alog, dsk, normw, s0, u0, consts, *, batch, seq, cb, mask_rows):
    nb = seq // (cb * CHUNK)
    rb = cb * CHUNK
    row = lambda b, c: b * nb + c
    tri, expand, psel, par, bmask = consts
    kern = functools.partial(_ssd_kernel, cb=cb, mask_rows=mask_rows)
    full2 = lambda shape: pl.BlockSpec(shape, lambda b, c: (0, 0))
    return pl.pallas_call(
        kern,
        grid=(batch, nb),
        in_specs=[
            pl.BlockSpec((rb, SSD_XBC), lambda b, c: (row(b, c), COL_XBC // SSD_XBC)),
            pl.BlockSpec((rb, SSD_D_INNER), lambda b, c: (row(b, c), COL_Z // SSD_D_INNER)),
            pl.BlockSpec((rb, SMALL_W), lambda b, c: (row(b, c), 0)),
            full2((SSD_CONV, SSD_XBC)),
            full2((1, SSD_XBC)),
            full2((1, SMALL_W)),
            full2((1, SMALL_W)),
            full2((1, SSD_D_INNER)),
            full2((1, SSD_D_INNER)),
            pl.BlockSpec((SSD_GROUPS, SSD_STATE, SSD_GW), lambda b, c: (0, 0, 0)),
            full2((8, SSD_XBC)),
            full2((CHUNK, CHUNK)),
            full2((SMALL_W, SSD_D_INNER)),
            full2((SSD_HEADS // 2, SMALL_W)),
            full2((2, SMALL_W)),
            full2((SSD_GW, SSD_GW)),
        ],
        out_specs=[
            pl.BlockSpec((rb, SSD_D_INNER), lambda b, c: (row(b, c), 0)),
            pl.BlockSpec((1, SSD_GROUPS, SSD_STATE, SSD_GW), lambda b, c: (b, 0, 0, 0)),
            pl.BlockSpec((1, 8, SSD_XBC), lambda b, c: (b, 0, 0)),
        ],
        out_shape=[
            jax.ShapeDtypeStruct((batch * seq, SSD_D_INNER), BF16),
            jax.ShapeDtypeStruct((batch, SSD_GROUPS, SSD_STATE, SSD_GW), F32),
            jax.ShapeDtypeStruct((batch, 8, SSD_XBC), F32),
        ],
        scratch_shapes=[pltpu.VMEM((SSD_GROUPS, SSD_STATE, SSD_GW), F32),
                        pltpu.VMEM((CHUNK + 8, SSD_XBC), F32),
                        pltpu.VMEM((CHUNK, SSD_XBC), F32)],
        compiler_params=pltpu.CompilerParams(
            dimension_semantics=("arbitrary", "arbitrary"), vmem_limit_bytes=V7X_VMEM_LIMIT),
        name="ssd_scan",
    )(proj, proj, small, cw, cbias, dtb, alog, dsk, normw, s0, u0, tri, expand, psel, par, bmask)


def _merge_kernel(og_ref, yg_ref, ga_ref, gb_ref, x_ref, wa_ref, wb_ref, wo_ref, lnw_ref, h_ref, hn_ref):
    ya = jnp.dot(og_ref[...], wa_ref[...], preferred_element_type=F32)
    yb = jnp.dot(yg_ref[...], wb_ref[...], preferred_element_type=F32)
    mixed = _sigmoid(ga_ref[...]) * ya + _sigmoid(gb_ref[...]) * yb
    h = x_ref[...] + _dot(mixed, wo_ref[...])
    h_ref[...] = h
    ms = jnp.mean(h * h, axis=-1, keepdims=True)
    hn_ref[...] = (h * lax.rsqrt(ms + RMS_EPS) * lnw_ref[...]).astype(BF16)


def _merge(og, yg, proj, x2d, wa, wb, wo, lnw):
    rows = x2d.shape[0]
    tm = min(512, rows)
    full = lambda shape: pl.BlockSpec(shape, lambda i: (0, 0))
    return pl.pallas_call(
        _merge_kernel,
        grid=(rows // tm,),
        in_specs=[
            pl.BlockSpec((tm, GLA_VW), lambda i: (i, 0)),
            pl.BlockSpec((tm, SSD_D_INNER), lambda i: (i, 0)),
            pl.BlockSpec((tm, D_MODEL), lambda i: (i, COL_GA // D_MODEL)),
            pl.BlockSpec((tm, D_MODEL), lambda i: (i, COL_GB // D_MODEL)),
            pl.BlockSpec((tm, D_MODEL), lambda i: (i, 0)),
            full((GLA_VW, D_MODEL)),
            full((SSD_D_INNER, D_MODEL)),
            full((D_MODEL, D_MODEL)),
            full((1, D_MODEL)),
        ],
        out_specs=[
            pl.BlockSpec((tm, D_MODEL), lambda i: (i, 0)),
            pl.BlockSpec((tm, D_MODEL), lambda i: (i, 0)),
        ],
        out_shape=[
            jax.ShapeDtypeStruct((rows, D_MODEL), F32),
            jax.ShapeDtypeStruct((rows, D_MODEL), BF16),
        ],
        compiler_params=pltpu.CompilerParams(
            dimension_semantics=("arbitrary",), vmem_limit_bytes=V7X_VMEM_LIMIT),
        name="merge",
    )(og, yg, proj, proj, x2d, wa, wb, wo, lnw)


PEER_LB = 512
PEER_CW = 128


def _topk_rows(vals, codes, k, out_v_ref, out_i_ref, row0, ls):
    for r in range(k):
        m = jnp.max(vals, axis=0, keepdims=True)
        idx = jnp.min(jnp.where(vals == m, codes, _CODE_NONE), axis=0, keepdims=True)
        out_v_ref[row0 + r:row0 + r + 1, ls] = m
        out_i_ref[row0 + r:row0 + r + 1, ls] = idx
        vals = jnp.where(codes == idx, -jnp.inf, vals)


_CODE_NONE = float(1 << 20)
PEER_NB = (16, 8, 5, 4, 3, 2, 2, 2)
PEER_CAND_ROWS = 16 + 8 * 7 + 8


def _cand_codes(lanes):
    r = lax.broadcasted_iota(jnp.int32, (PEER_CAND_ROWS, lanes), 0)
    mid = r - 16
    code_mid = (((mid >> 3) + 1) << 4) + (mid & 7)
    code_last = (r - 72 + 8) << 4
    return jnp.where(r < 16, r, jnp.where(r < 72, code_mid, code_last)).astype(F32)


def _select_kernel(hn_ref, wq_ref, keys_ref, i_ref, j_ref, w_ref,
                   tv_ref, ti_ref, cand_ref, bv_ref, bp_ref, oi_ref, oj_ref, ow_ref):
    h = pl.program_id(1)
    hn = hn_ref[...]
    scores = []
    for half in range(2):
        qt = lax.dot_general(wq_ref[half * PEER_HALF:(half + 1) * PEER_HALF, :], hn,
                             (((1,), (1,)), ((), ())), preferred_element_type=F32)
        scores.append(jnp.dot(keys_ref[0, half], qt.astype(BF16), preferred_element_type=F32))
    key_codes = lax.broadcasted_iota(jnp.int32, (PEER_N_KEYS, PEER_CW), 0).astype(F32)
    cand_codes = _cand_codes(PEER_CW)
    sub8 = lax.broadcasted_iota(jnp.int32, (8, PEER_CW), 0)
    for lb in range(hn.shape[0] // PEER_CW):
        ls = slice(lb * PEER_CW, (lb + 1) * PEER_CW)
        for half in range(2):
            _topk_rows(scores[half][:, ls], key_codes, PEER_TOPK, tv_ref, ti_ref, half * PEER_TOPK, ls)
        tv1 = tv_ref[PEER_TOPK:2 * PEER_TOPK, ls]
        cand_ref[0:PEER_TOPK, ls] = tv_ref[0:1, ls] + tv1
        for a in range(1, 8):
            row = 16 + 8 * (a - 1)
            cand_ref[row:row + 8, ls] = jnp.where(sub8 < PEER_NB[a], tv_ref[a:a + 1, ls] + tv1[0:8, :], -jnp.inf)
        cand_ref[72:80, ls] = tv_ref[8:PEER_TOPK, ls] + tv1[0:1, :]
        _topk_rows(cand_ref[:, ls], cand_codes, PEER_TOPK, bv_ref, bp_ref, 0, ls)
    best = bv_ref[...]
    pos = bp_ref[...].astype(jnp.int32)
    a_sel = pos >> 4
    b_sel = pos & (PEER_TOPK - 1)
    isel = jnp.zeros_like(best)
    jsel = jnp.zeros_like(best)
    for a in range(PEER_TOPK):
        isel = jnp.where(a_sel == a, ti_ref[a:a + 1, :], isel)
        jsel = jnp.where(b_sel == a, ti_ref[PEER_TOPK + a:PEER_TOPK + a + 1, :], jsel)
    e = jnp.exp(best - best[0:1, :])
    wgt = e / jnp.sum(e, axis=0, keepdims=True)
    hs = pl.ds(pl.multiple_of(h * PEER_TOPK, PEER_TOPK), PEER_TOPK)
    oi_ref[hs, :] = isel
    oj_ref[hs, :] = jsel
    ow_ref[hs, :] = wgt

    @pl.when(h == PEER_HEADS - 1)
    def _():
        i_ref[...] = oi_ref[...].T
        j_ref[...] = oj_ref[...].T
        w_ref[...] = ow_ref[...].T


def _select(hn2, wq_t, keys):
    t = hn2.shape[0]
    tn = PEER_LB
    out = jax.ShapeDtypeStruct((t, PEER_SLOTS), F32)
    return pl.pallas_call(
        _select_kernel,
        grid=(t // tn, PEER_HEADS),
        in_specs=[
            pl.BlockSpec((tn, D_MODEL), lambda i, h: (i, 0)),
            pl.BlockSpec((2 * PEER_HALF, D_MODEL), lambda i, h: (h, 0)),
            pl.BlockSpec((1, 2, PEER_N_KEYS, PEER_HALF), lambda i, h: (h, 0, 0, 0)),
        ],
        out_specs=[pl.BlockSpec((tn, PEER_SLOTS), lambda i, h: (i, 0))] * 3,
        out_shape=[out, out, out],
        scratch_shapes=[
            pltpu.VMEM((2 * PEER_TOPK, tn), F32),
            pltpu.VMEM((2 * PEER_TOPK, tn), F32),
            pltpu.VMEM((PEER_CAND_ROWS, tn), F32),
            pltpu.VMEM((PEER_TOPK, tn), F32),
            pltpu.VMEM((PEER_TOPK, tn), F32),
            pltpu.VMEM((PEER_SLOTS, tn), F32),
            pltpu.VMEM((PEER_SLOTS, tn), F32),
            pltpu.VMEM((PEER_SLOTS, tn), F32),
        ],
        compiler_params=pltpu.CompilerParams(
            dimension_semantics=("arbitrary", "arbitrary"), vmem_limit_bytes=V7X_VMEM_LIMIT),
        name="peer_select",
    )(hn2, wq_t, keys)


PEER_TN = 512
PEER_EB = 1024
PEER_IH = 128


def _gelu_tanh(x):
    return 0.5 * x * (1.0 + jnp.tanh(math.sqrt(2.0 / math.pi) * (x + 0.044715 * (x * x * x))))


def _expert_kernel(hn_ref, i_ref, j_ref, w_ref, h_ref, u_ref, v_ref, lnw_ref, o_ref, r_ref, *, tn, pitch):
    e = pl.program_id(1)
    steps_per_half = (PEER_IH * PEER_N_KEYS) // PEER_EB
    ipb = PEER_EB // PEER_N_KEYS

    @pl.when(e == 0)
    def _():
        o_ref[...] = jnp.zeros_like(o_ref)

    @pl.when(e % steps_per_half == 0)
    def _():
        i_base = (e // steps_per_half) * PEER_IH
        sub = (lax.broadcasted_iota(jnp.int32, (PEER_IH, PEER_SLOTS), 0) + i_base).astype(F32)
        subj = lax.broadcasted_iota(jnp.int32, (PEER_N_KEYS, PEER_SLOTS), 0).astype(F32)

        def body(n, carry):
            irow = i_ref[pl.ds(n, 1), :]
            jrow = j_ref[pl.ds(n, 1), :]
            wrow = w_ref[pl.ds(n, 1), :]
            w_hi = wrow.astype(BF16)
            w_lo = (wrow - w_hi.astype(F32)).astype(BF16)
            hit = sub == irow
            q_hi = jnp.where(hit, w_hi.astype(F32), 0.0).astype(BF16)
            q_lo = jnp.where(hit, w_lo.astype(F32), 0.0).astype(BF16)
            p_t = jnp.where(subj == jrow, 1.0, 0.0).T.astype(BF16)
            r = (jnp.dot(q_hi, p_t, preferred_element_type=F32)
                 + jnp.dot(q_lo, p_t, preferred_element_type=F32))
            r_ref[pl.ds(n, PEER_IH, stride=pitch), :] = r
            return carry

        lax.fori_loop(0, tn, body, 0, unroll=32)

    i_loc = (e % steps_per_half) * ipb
    act = lax.dot_general(hn_ref[...], u_ref[...], (((1,), (1,)), ((), ())),
                          preferred_element_type=F32)
    wblk = jnp.concatenate(
        [r_ref[pl.ds(pl.multiple_of((i_loc + ii) * pitch, 8), tn), :] for ii in range(ipb)], axis=1)
    g = (_gelu_tanh(act) * wblk).astype(BF16)
    o_ref[...] += jnp.dot(g, v_ref[...], preferred_element_type=F32)

    @pl.when(e == pl.num_programs(1) - 1)
    def _():
        h = h_ref[...] + o_ref[...]
        ms = jnp.mean(h * h, axis=-1, keepdims=True)
        o_ref[...] = h * lax.rsqrt(ms + RMS_EPS) * lnw_ref[...]


def _experts(hn2, isel, jsel, wsel, h2, u_tab, v_tab, lnw):
    t = hn2.shape[0]
    tn = min(PEER_TN, t)
    pitch = tn + 8
    n_exp = u_tab.shape[0]
    kern = functools.partial(_expert_kernel, tn=tn, pitch=pitch)
    return pl.pallas_call(
        kern,
        grid=(t // tn, n_exp // PEER_EB),
        in_specs=[
            pl.BlockSpec((tn, D_MODEL), lambda i, e: (i, 0)),
            pl.BlockSpec((tn, PEER_SLOTS), lambda i, e: (i, 0)),
            pl.BlockSpec((tn, PEER_SLOTS), lambda i, e: (i, 0)),
            pl.BlockSpec((tn, PEER_SLOTS), lambda i, e: (i, 0)),
            pl.BlockSpec((tn, D_MODEL), lambda i, e: (i, 0)),
            pl.BlockSpec((PEER_EB, D_MODEL), lambda i, e: (e, 0)),
            pl.BlockSpec((PEER_EB, D_MODEL), lambda i, e: (e, 0)),
            pl.BlockSpec((1, D_MODEL), lambda i, e: (0, 0)),
        ],
        out_specs=pl.BlockSpec((tn, D_MODEL), lambda i, e: (i, 0)),
        out_shape=jax.ShapeDtypeStruct((t, D_MODEL), F32),
        scratch_shapes=[pltpu.VMEM((PEER_IH * pitch, PEER_N_KEYS), F32)],
        compiler_params=pltpu.CompilerParams(
            dimension_semantics=("arbitrary", "arbitrary"), vmem_limit_bytes=V7X_VMEM_LIMIT_EXPERTS),
        name="peer_experts",
    )(hn2, isel, jsel, wsel, h2, u_tab, v_tab, lnw)


def _constants():
    t = jnp.arange(CHUNK)
    tri = (t[None, :] <= t[:, None]).astype(BF16)
    lane = jnp.arange(SSD_D_INNER)
    small_lane = jnp.arange(SMALL_W)
    head_of_lane = small_lane - SMALL_DT
    is_head = (head_of_lane >= 0) & (head_of_lane < SSD_HEADS)
    expand = (small_lane[:, None] == SMALL_DT + lane[None, :] // SSD_HEADDIM).astype(BF16)
    psel = (is_head[None, :] & (head_of_lane[None, :] // 2 == jnp.arange(SSD_HEADS // 2)[:, None])).astype(BF16)
    par = (is_head[None, :] & (head_of_lane[None, :] % 2 == jnp.arange(2)[:, None])).astype(BF16)
    gl = jnp.arange(SSD_GW)
    bmask = (gl[:, None] // CHUNK == gl[None, :] // SSD_HEADDIM).astype(BF16)
    return tri, expand, psel, par, bmask


def _pack_in_weights(w_in):
    o = 0
    parts = {}
    for name, width in (("q", GLA_QK), ("k", GLA_QK), ("v", GLA_VW), ("gout", GLA_VW), ("glow", GLA_GATE_RANK),
                        ("z", SSD_D_INNER), ("xbc", SSD_XBC), ("dt", SSD_HEADS), ("ga", D_MODEL), ("gb", D_MODEL)):
        parts[name] = w_in[:, o:o + width]
        o += width
    main = jnp.concatenate([parts[n] for n in ("q", "k", "v", "z", "gout", "ga", "xbc", "gb")], axis=1)
    small = jnp.concatenate(
        [parts["glow"], parts["dt"], jnp.zeros((D_MODEL, SMALL_W - GLA_GATE_RANK - SSD_HEADS), w_in.dtype)], axis=1)
    return main.astype(BF16), small.astype(BF16)


def _pad_small(vec, offset):
    out = jnp.zeros((1, SMALL_W), F32)
    return out.at[0, offset:offset + vec.shape[0]].set(vec.astype(F32))


def kernel(x, meta_tokens, ln_mix_w, w_in, gla_w_gate2, gla_b_gate, gla_norm_w, ssd_conv_w, ssd_conv_b,
           ssd_dt_bias, ssd_a_log, ssd_d, ssd_norm_w, w_up_gla, w_up_ssd, w_out, ln_ffn_w,
           peer_w_q, peer_sub_keys, peer_u, peer_v, ln_final_w):
    bsz, seq, _ = x.shape
    assert seq % (4 * CHUNK) == 0
    l = 0
    consts = _constants()
    tri = consts[0]

    w_main, w_small = _pack_in_weights(w_in[l])
    lnw = ln_mix_w[l].reshape(1, D_MODEL)
    w2p = jnp.zeros((SMALL_W, GLA_QK), F32).at[SMALL_GLOW:SMALL_GLOW + GLA_GATE_RANK].set(gla_w_gate2[l]).astype(BF16)
    bgate = gla_b_gate[l].reshape(1, GLA_QK)
    gnw = gla_norm_w[l].reshape(1, GLA_DV)
    cw = ssd_conv_w[l]
    cbias = ssd_conv_b[l].reshape(1, SSD_XBC)
    dtb = _pad_small(ssd_dt_bias[l], SMALL_DT)
    alog = _pad_small(ssd_a_log[l], SMALL_DT)
    dsk = jnp.repeat(ssd_d[l].astype(F32), SSD_HEADDIM).reshape(1, SSD_D_INNER)
    snw = ssd_norm_w[l].reshape(1, SSD_D_INNER)

    meta_rows = jnp.concatenate([jnp.zeros((META_PAD, D_MODEL), F32), meta_tokens.astype(F32)], axis=0)
    proj_m, small_m = _inproj(meta_rows, lnw, w_main, w_small)
    gla_s0 = jnp.zeros((GLA_HEADS, GLA_DV, GLA_DK), F32)
    ssd_s0 = jnp.zeros((SSD_GROUPS, SSD_STATE, SSD_GW), F32)
    u0 = jnp.zeros((8, SSD_XBC), F32)
    _, gla_s1 = _gla(proj_m, small_m, w2p, bgate, gnw, gla_s0, tri, batch=1, seq=CHUNK, cb=1, mask_rows=META_PAD)
    _, ssd_s1, u1 = _ssd(proj_m, small_m, cw, cbias, dtb, alog, dsk, snw, ssd_s0, u0, consts,
                         batch=1, seq=CHUNK, cb=1, mask_rows=META_PAD)

    x2d = x.reshape(bsz * seq, D_MODEL)
    proj, small = _inproj(x2d, lnw, w_main, w_small)
    og, _ = _gla(proj, small, w2p, bgate, gnw, gla_s1[0], tri, batch=bsz, seq=seq, cb=4, mask_rows=0)
    yg, _, _ = _ssd(proj, small, cw, cbias, dtb, alog, dsk, snw, ssd_s1[0], u1[0], consts,
                    batch=bsz, seq=seq, cb=4, mask_rows=0)
    h2, hn2 = _merge(og, yg, proj, x2d, w_up_gla[l].astype(BF16), w_up_ssd[l].astype(BF16),
                     w_out[l].astype(BF16), ln_ffn_w[l].reshape(1, D_MODEL))

    wq_t = peer_w_q[l].T.astype(BF16)
    keys = peer_sub_keys[l].astype(BF16)
    isel, jsel, wsel = _select(hn2, wq_t, keys)
    out = _experts(hn2, isel, jsel, wsel, h2, peer_u[l].astype(BF16), peer_v[l].astype(BF16),
                   ln_final_w.reshape(1, D_MODEL))
    return out.reshape(bsz, seq, D_MODEL)
```

```python
import functools
import math

import jax
import jax.numpy as jnp
from jax import lax
from jax.experimental import pallas as pl
from jax.experimental.pallas import tpu as pltpu

F32 = jnp.float32
BF16 = jnp.bfloat16

D_MODEL = 1024
CHUNK = 64
N_META = 16
META_PAD = CHUNK - N_META
RMS_EPS = 1e-6
GLA_HEADS = 4
GLA_DK = 128
GLA_DV = 256
GLA_GATE_RANK = 16
GLA_GATE_NORM = 16.0
GLA_QK = 512
GLA_VW = 1024
SSD_D_INNER = 2048
SSD_HEADDIM = 64
SSD_HEADS = 32
SSD_GROUPS = 4
SSD_HPG = 8
SSD_STATE = 128
SSD_CONV = 4
SSD_BC = 512
SSD_XBC = 3072
SSD_GW = SSD_D_INNER // SSD_GROUPS
PEER_HEADS = 8
PEER_N_KEYS = 128
PEER_HALF = 128
PEER_TOPK = 16
PEER_SLOTS = PEER_HEADS * PEER_TOPK

COL_Q, COL_K, COL_V, COL_Z, COL_GOUT, COL_GA, COL_XBC, COL_GB = 0, 512, 1024, 2048, 4096, 5120, 6144, 9216
MAIN_W = 10240
SMALL_W = 128
SMALL_GLOW = 0
SMALL_DT = 16

V7X_VMEM_LIMIT = 56 * 1024 * 1024
V7X_VMEM_LIMIT_EXPERTS = 60 * 1024 * 1024


def _dot(a, b):
    return jnp.dot(a.astype(BF16), b.astype(BF16), preferred_element_type=F32)


def _dot_nt(a, b):
    return lax.dot_general(a.astype(BF16), b.astype(BF16), (((1,), (1,)), ((), ())),
                           preferred_element_type=F32)


def _dot_tn(a, b):
    return lax.dot_general(a.astype(BF16), b.astype(BF16), (((0,), (0,)), ((), ())),
                           preferred_element_type=F32)


def _split3(x):
    x1 = x.astype(BF16)
    r = x - x1.astype(F32)
    x2 = r.astype(BF16)
    x3 = (r - x2.astype(F32)).astype(BF16)
    return x1, x2, x3


def _dot_exact_l(m, x):
    x1, x2, x3 = _split3(x)
    return (jnp.dot(m, x1, preferred_element_type=F32) + jnp.dot(m, x2, preferred_element_type=F32)
            + jnp.dot(m, x3, preferred_element_type=F32))


def _sigmoid(x):
    return 0.5 * jnp.tanh(0.5 * x) + 0.5


def _silu(x):
    return x * _sigmoid(x)


def _softplus(x):
    return jnp.maximum(x, 0.0) + jnp.log1p(jnp.exp(-jnp.abs(x)))


def _inproj_kernel(x_ref, lnw_ref, w_ref, ws_ref, o_ref, os_ref, hn_ref):
    @pl.when(pl.program_id(1) == 0)
    def _():
        x = x_ref[...]
        ms = jnp.mean(x * x, axis=-1, keepdims=True)
        hn_ref[...] = (x * lax.rsqrt(ms + RMS_EPS) * lnw_ref[...]).astype(BF16)
        os_ref[...] = jnp.dot(hn_ref[...], ws_ref[...], preferred_element_type=F32)

    o_ref[...] = jnp.dot(hn_ref[...], w_ref[...], preferred_element_type=F32)


def _inproj(x2d, lnw, w_main, w_small):
    rows = x2d.shape[0]
    tm = min(1024, rows)
    tn = 1024
    return pl.pallas_call(
        _inproj_kernel,
        grid=(rows // tm, MAIN_W // tn),
        in_specs=[
            pl.BlockSpec((tm, D_MODEL), lambda i, j: (i, 0)),
            pl.BlockSpec((1, D_MODEL), lambda i, j: (0, 0)),
            pl.BlockSpec((D_MODEL, tn), lambda i, j: (0, j)),
            pl.BlockSpec((D_MODEL, SMALL_W), lambda i, j: (0, 0)),
        ],
        out_specs=[
            pl.BlockSpec((tm, tn), lambda i, j: (i, j)),
            pl.BlockSpec((tm, SMALL_W), lambda i, j: (i, 0)),
        ],
        out_shape=[
            jax.ShapeDtypeStruct((rows, MAIN_W), F32),
            jax.ShapeDtypeStruct((rows, SMALL_W), F32),
        ],
        scratch_shapes=[pltpu.VMEM((tm, D_MODEL), BF16)],
        compiler_params=pltpu.CompilerParams(
            dimension_semantics=("arbitrary", "arbitrary"), vmem_limit_bytes=V7X_VMEM_LIMIT),
        name="inproj",
    )(x2d, lnw, w_main, w_small)


def _gla_kernel(q_ref, k_ref, v_ref, g_ref, sm_ref, w2_ref, bg_ref, nw_ref, s0_ref, tri_ref,
                og_ref, sfin_ref, s_ref, *, cb, mask_rows):
    c = pl.program_id(1)

    @pl.when(c == 0)
    def _():
        s_ref[...] = s0_ref[...]

    tri = tri_ref[...]
    tri_b = (lax.broadcasted_iota(jnp.int32, (CHUNK, CHUNK), 1)
             <= lax.broadcasted_iota(jnp.int32, (CHUNK, CHUNK), 0))
    scale = GLA_DK ** -0.5

    def chunk_step(ci, carry):
        rs = pl.ds(pl.multiple_of(ci * CHUNK, CHUNK), CHUNK)
        lg = _dot(sm_ref[rs, :], w2_ref[...]) + bg_ref[...]
        lg = (jnp.minimum(lg, 0.0) - jnp.log1p(jnp.exp(-jnp.abs(lg)))) * (1.0 / GLA_GATE_NORM)
        if mask_rows:
            keep = lax.broadcasted_iota(jnp.int32, (CHUNK, GLA_QK), 0) >= mask_rows
            lg = jnp.where(keep, lg, 0.0)
        gcum = _dot_exact_l(tri, lg)
        for h in range(GLA_HEADS):
            ks = slice(h * GLA_DK, (h + 1) * GLA_DK)
            vs = slice(h * GLA_DV, (h + 1) * GLA_DV)
            qh = q_ref[rs, ks] * scale
            kh = k_ref[rs, ks]
            vh = v_ref[rs, vs]
            if mask_rows:
                keep_k = lax.broadcasted_iota(jnp.int32, (CHUNK, GLA_DK), 0) >= mask_rows
                keep_v = lax.broadcasted_iota(jnp.int32, (CHUNK, GLA_DV), 0) >= mask_rows
                qh = jnp.where(keep_k, qh, 0.0)
                kh = jnp.where(keep_k, kh, 0.0)
                vh = jnp.where(keep_v, vh, 0.0)
            gh = gcum[:, ks]
            eg = jnp.exp(gh)
            egn = jnp.exp(-gh)
            qe = qh * eg
            a_causal = _dot_nt(qe, kh * egn)
            a_ahead = _dot_nt(qh * egn, kh * eg)
            att = jnp.where(tri_b, a_causal, a_ahead)
            st = s_ref[h]
            o = _dot(att, vh) + _dot_nt(qe, st)
            glast = gh[CHUNK - 1:CHUNK, :]
            kd = kh * jnp.exp(glast - gh)
            s_ref[h] = st * jnp.exp(glast) + _dot_tn(vh, kd)
            ms = jnp.mean(o * o, axis=-1, keepdims=True)
            on = o * lax.rsqrt(ms + RMS_EPS) * nw_ref[...]
            og_ref[rs, vs] = (on * _silu(g_ref[rs, vs])).astype(og_ref.dtype)
        return carry

    lax.fori_loop(0, cb, chunk_step, 0, unroll=2)

    @pl.when(c == pl.num_programs(1) - 1)
    def _():
        sfin_ref[0] = s_ref[...]


def _gla(proj, small, w2p, bgate, normw, s0, tri, *, batch, seq, cb, mask_rows):
    nb = seq // (cb * CHUNK)
    rb = cb * CHUNK
    row = lambda b, c: b * nb + c
    kern = functools.partial(_gla_kernel, cb=cb, mask_rows=mask_rows)
    return pl.pallas_call(
        kern,
        grid=(batch, nb),
        in_specs=[
            pl.BlockSpec((rb, GLA_QK), lambda b, c: (row(b, c), COL_Q // GLA_QK)),
            pl.BlockSpec((rb, GLA_QK), lambda b, c: (row(b, c), COL_K // GLA_QK)),
            pl.BlockSpec((rb, GLA_VW), lambda b, c: (row(b, c), COL_V // GLA_VW)),
            pl.BlockSpec((rb, GLA_VW), lambda b, c: (row(b, c), COL_GOUT // GLA_VW)),
            pl.BlockSpec((rb, SMALL_W), lambda b, c: (row(b, c), 0)),
            pl.BlockSpec((SMALL_W, GLA_QK), lambda b, c: (0, 0)),
            pl.BlockSpec((1, GLA_QK), lambda b, c: (0, 0)),
            pl.BlockSpec((1, GLA_DV), lambda b, c: (0, 0)),
            pl.BlockSpec((GLA_HEADS, GLA_DV, GLA_DK), lambda b, c: (0, 0, 0)),
            pl.BlockSpec((CHUNK, CHUNK), lambda b, c: (0, 0)),
        ],
        out_specs=[
            pl.BlockSpec((rb, GLA_VW), lambda b, c: (row(b, c), 0)),
            pl.BlockSpec((1, GLA_HEADS, GLA_DV, GLA_DK), lambda b, c: (b, 0, 0, 0)),
        ],
        out_shape=[
            jax.ShapeDtypeStruct((batch * seq, GLA_VW), BF16),
            jax.ShapeDtypeStruct((batch, GLA_HEADS, GLA_DV, GLA_DK), F32),
        ],
        scratch_shapes=[pltpu.VMEM((GLA_HEADS, GLA_DV, GLA_DK), F32)],
        compiler_params=pltpu.CompilerParams(
            dimension_semantics=("arbitrary", "arbitrary"), vmem_limit_bytes=V7X_VMEM_LIMIT),
        name="gla_scan",
    )(proj, proj, proj, proj, small, w2p, bgate, normw, s0, tri)


def _ssd_kernel(xbc_ref, z_ref, sm_ref, cw_ref, cbias_ref, dtb_ref, alog_ref, dsk_ref, nw_ref,
                s0_ref, u0_ref, tri_ref, exp_ref, psel_ref, par_ref, bmask_ref,
                yg_ref, sfin_ref, ufin_ref, s_ref, ext_ref, xc_ref, *, cb, mask_rows):
    c = pl.program_id(1)

    @pl.when(c == 0)
    def _():
        s_ref[...] = s0_ref[...]
        ext_ref[0:8, :] = u0_ref[...]

    tri = tri_ref[...]
    psel = psel_ref[...]
    par0 = par_ref[0:1, :]
    par1 = par_ref[1:2, :]
    lane_blk = 512

    def chunk_step(ci, carry):
        rs = pl.ds(pl.multiple_of(ci * CHUNK, CHUNK), CHUNK)
        for lb in range(SSD_XBC // lane_blk):
            ls = slice(lb * lane_blk, (lb + 1) * lane_blk)
            ext_ref[8:8 + CHUNK, ls] = xbc_ref[rs, ls]
            conv = cbias_ref[:, ls] + ext_ref[8:8 + CHUNK, ls] * cw_ref[SSD_CONV - 1:SSD_CONV, ls]
            for j in range(1, SSD_CONV):
                conv = conv + ext_ref[8 - j:8 - j + CHUNK, ls] * cw_ref[SSD_CONV - 1 - j:SSD_CONV - j, ls]
            xc = _silu(conv)
            if mask_rows:
                xc = jnp.where(lax.broadcasted_iota(jnp.int32, (CHUNK, lane_blk), 0) >= mask_rows, xc, 0.0)
            xc_ref[:, ls] = xc
            ext_ref[0:8, ls] = ext_ref[CHUNK:CHUNK + 8, ls]
        dt = _softplus(sm_ref[rs, :] + dtb_ref[...])
        if mask_rows:
            dt = jnp.where(lax.broadcasted_iota(jnp.int32, (CHUNK, SMALL_W), 0) >= mask_rows, dt, 0.0)
        a = -dt * jnp.exp(alog_ref[...])
        acum = _dot_exact_l(tri, a)
        acum_parts = _split3(acum)
        parts6 = jnp.concatenate(acum_parts + _split3(dt), axis=0)
        acum_t = sum(
            lax.dot_general(psel, jnp.concatenate([x * par0, x * par1], axis=0), (((1,), (1,)), ((), ())),
                            preferred_element_type=F32) for x in acum_parts)
        for g in range(SSD_GROUPS):
            gs = slice(g * SSD_GW, (g + 1) * SSD_GW)
            expand = exp_ref[:, gs]
            ex = jnp.dot(parts6, expand, preferred_element_type=F32)
            colb = ex[0:CHUNK] + ex[CHUNK:2 * CHUNK] + ex[2 * CHUNK:3 * CHUNK]
            dtx = ex[3 * CHUNK:4 * CHUNK] + ex[4 * CHUNK:5 * CHUNK] + ex[5 * CHUNK:]
            rowb = jnp.concatenate(
                [jnp.broadcast_to(acum_t[4 * g + k:4 * g + k + 1, :], (CHUNK, 2 * CHUNK)) for k in range(4)],
                axis=1)
            seg = jnp.exp(-jnp.abs(colb - rowb))
            clast = colb[CHUNK - 1:CHUNK, :]
            xs = xc_ref[:, gs]
            bg = xc_ref[:, SSD_D_INNER + g * SSD_STATE:SSD_D_INNER + (g + 1) * SSD_STATE]
            cg = xc_ref[:, SSD_D_INNER + SSD_BC + g * SSD_STATE:SSD_D_INNER + SSD_BC + (g + 1) * SSD_STATE]
            xg = xs * dtx
            cbx = _dot_nt(cg, jnp.concatenate([bg] * SSD_HPG, axis=0))
            sc = seg * cbx
            bd = jnp.concatenate([xg.astype(BF16)] * SSD_HPG, axis=0) * bmask_ref[...]
            st = s_ref[g]
            y = _dot(sc, bd) + _dot(cg, st) * jnp.exp(colb)
            s_ref[g] = st * jnp.exp(clast) + _dot_tn(bg, xg * jnp.exp(clast - colb))
            y = y + dsk_ref[:, gs] * xs
            y = y * _silu(z_ref[rs, gs])
            ms = jnp.mean(y * y, axis=-1, keepdims=True)
            yg_ref[rs, gs] = (y * lax.rsqrt(ms + RMS_EPS) * nw_ref[:, gs]).astype(yg_ref.dtype)
        return carry

    lax.fori_loop(0, cb, chunk_step, 0, unroll=2)

    @pl.when(c == pl.num_programs(1) - 1)
    def _():
        sfin_ref[0] = s_ref[...]
        ufin_ref[0] = ext_ref[0:8, :]


def _ssd(proj, small, cw, cbias, dtb, alog, dsk, normw, s0, u0, consts, *, batch, seq, cb, mask_rows):
    nb = seq // (cb * CHUNK)
    rb = cb * CHUNK
    row = lambda b, c: b * nb + c
    tri, expand, psel, par, bmask = consts
    kern = functools.partial(_ssd_kernel, cb=cb, mask_rows=mask_rows)
    full2 = lambda shape: pl.BlockSpec(shape, lambda b, c: (0, 0))
    return pl.pallas_call(
        kern,
        grid=(batch, nb),
        in_specs=[
            pl.BlockSpec((rb, SSD_XBC), lambda b, c: (row(b, c), COL_XBC // SSD_XBC)),
            pl.BlockSpec((rb, SSD_D_INNER), lambda b, c: (row(b, c), COL_Z // SSD_D_INNER)),
            pl.BlockSpec((rb, SMALL_W), lambda b, c: (row(b, c), 0)),
            full2((SSD_CONV, SSD_XBC)),
            full2((1, SSD_XBC)),
            full2((1, SMALL_W)),
            full2((1, SMALL_W)),
            full2((1, SSD_D_INNER)),
            full2((1, SSD_D_INNER)),
            pl.BlockSpec((SSD_GROUPS, SSD_STATE, SSD_GW), lambda b, c: (0, 0, 0)),
            full2((8, SSD_XBC)),
            full2((CHUNK, CHUNK)),
            full2((SMALL_W, SSD_D_INNER)),
            full2((SSD_HEADS // 2, SMALL_W)),
            full2((2, SMALL_W)),
            full2((SSD_GW, SSD_GW)),
        ],
        out_specs=[
            pl.BlockSpec((rb, SSD_D_INNER), lambda b, c: (row(b, c), 0)),
            pl.BlockSpec((1, SSD_GROUPS, SSD_STATE, SSD_GW), lambda b, c: (b, 0, 0, 0)),
            pl.BlockSpec((1, 8, SSD_XBC), lambda b, c: (b, 0, 0)),
        ],
        out_shape=[
            jax.ShapeDtypeStruct((batch * seq, SSD_D_INNER), BF16),
            jax.ShapeDtypeStruct((batch, SSD_GROUPS, SSD_STATE, SSD_GW), F32),
            jax.ShapeDtypeStruct((batch, 8, SSD_XBC), F32),
        ],
        scratch_shapes=[pltpu.VMEM((SSD_GROUPS, SSD_STATE, SSD_GW), F32),
                        pltpu.VMEM((CHUNK + 8, SSD_XBC), F32),
                        pltpu.VMEM((CHUNK, SSD_XBC), F32)],
        compiler_params=pltpu.CompilerParams(
            dimension_semantics=("arbitrary", "arbitrary"), vmem_limit_bytes=V7X_VMEM_LIMIT),
        name="ssd_scan",
    )(proj, proj, small, cw, cbias, dtb, alog, dsk, normw, s0, u0, tri, expand, psel, par, bmask)


def _merge_kernel(og_ref, yg_ref, ga_ref, gb_ref, x_ref, wa_ref, wb_ref, wo_ref, lnw_ref, h_ref, hn_ref):
    ya = jnp.dot(og_ref[...], wa_ref[...], preferred_element_type=F32)
    yb = jnp.dot(yg_ref[...], wb_ref[...], preferred_element_type=F32)
    mixed = _sigmoid(ga_ref[...]) * ya + _sigmoid(gb_ref[...]) * yb
    h = x_ref[...] + _dot(mixed, wo_ref[...])
    h_ref[...] = h
    ms = jnp.mean(h * h, axis=-1, keepdims=True)
    hn_ref[...] = (h * lax.rsqrt(ms + RMS_EPS) * lnw_ref[...]).astype(BF16)


def _merge(og, yg, proj, x2d, wa, wb, wo, lnw):
    rows = x2d.shape[0]
    tm = min(512, rows)
    full = lambda shape: pl.BlockSpec(shape, lambda i: (0, 0))
    return pl.pallas_call(
        _merge_kernel,
        grid=(rows // tm,),
        in_specs=[
            pl.BlockSpec((tm, GLA_VW), lambda i: (i, 0)),
            pl.BlockSpec((tm, SSD_D_INNER), lambda i: (i, 0)),
            pl.BlockSpec((tm, D_MODEL), lambda i: (i, COL_GA // D_MODEL)),
            pl.BlockSpec((tm, D_MODEL), lambda i: (i, COL_GB // D_MODEL)),
            pl.BlockSpec((tm, D_MODEL), lambda i: (i, 0)),
            full((GLA_VW, D_MODEL)),
            full((SSD_D_INNER, D_MODEL)),
            full((D_MODEL, D_MODEL)),
            full((1, D_MODEL)),
        ],
        out_specs=[
            pl.BlockSpec((tm, D_MODEL), lambda i: (i, 0)),
            pl.BlockSpec((tm, D_MODEL), lambda i: (i, 0)),
        ],
        out_shape=[
            jax.ShapeDtypeStruct((rows, D_MODEL), F32),
            jax.ShapeDtypeStruct((rows, D_MODEL), BF16),
        ],
        compiler_params=pltpu.CompilerParams(
            dimension_semantics=("arbitrary",), vmem_limit_bytes=V7X_VMEM_LIMIT),
        name="merge",
    )(og, yg, proj, proj, x2d, wa, wb, wo, lnw)


PEER_TN = 512
PEER_EB = 1024
PEER_CW = 128


def _topk_rows(vals, codes, k, write):
    for r in range(k):
        m = jnp.max(vals, axis=0, keepdims=True)
        idx = jnp.min(jnp.where(vals == m, codes, _CODE_NONE), axis=0, keepdims=True)
        write(r, m, idx)
        vals = jnp.where(codes == idx, -jnp.inf, vals)


_CODE_NONE = float(1 << 20)
PEER_NB = (16, 8, 5, 4, 3, 2, 2, 2)
PEER_CAND_ROWS = 16 + 8 * 7 + 8


def _cand_codes(lanes):
    r = lax.broadcasted_iota(jnp.int32, (PEER_CAND_ROWS, lanes), 0)
    mid = r - 16
    code_mid = (((mid >> 3) + 1) << 4) + (mid & 7)
    code_last = (r - 72 + 8) << 4
    return jnp.where(r < 16, r, jnp.where(r < 72, code_mid, code_last)).astype(F32)


def _gelu_tanh(x):
    return 0.5 * x * (1.0 + jnp.tanh(math.sqrt(2.0 / math.pi) * (x + 0.044715 * (x * x * x))))


def _select_level1(e, hn_ref, wq_ref, keys_ref, tv_ref, ti_ref):
    half = e % 2
    lanes = hn_ref.shape[0]
    qt = lax.dot_general(wq_ref[...], hn_ref[...], (((1,), (1,)), ((), ())),
                         preferred_element_type=F32)
    s = jnp.dot(keys_ref[0, 0], qt.astype(BF16), preferred_element_type=F32)
    key_codes = lax.broadcasted_iota(jnp.int32, (PEER_N_KEYS, PEER_CW), 0).astype(F32)
    chain_zero = None
    for lb in range(lanes // PEER_CW):
        ls = slice(lb * PEER_CW, (lb + 1) * PEER_CW)
        last = []

        def write1(r, m, idx, ls=ls, last=last):
            tv_ref[half, r:r + 1, ls] = m
            ti_ref[half, r:r + 1, ls] = idx
            last[:] = [m]

        blk = s[:, ls]
        if chain_zero is not None:
            blk = blk + chain_zero
        _topk_rows(blk, key_codes, PEER_TOPK, write1)
        chain_zero = jnp.minimum(last[0], 0.0) * 0.0


def _select_level2(e, tv_ref, ti_ref, cand_ref, bv_ref, bp_ref, oi_ref, oj_ref, ow_ref):
    head = e // 2
    lanes = tv_ref.shape[2]

    @pl.when(e % 2 == 1)
    def _():
        cand_codes = _cand_codes(PEER_CW)
        sub8 = lax.broadcasted_iota(jnp.int32, (8, PEER_CW), 0)
        for lb in range(lanes // PEER_CW):
            ls = slice(lb * PEER_CW, (lb + 1) * PEER_CW)
            tv1 = tv_ref[1, :, ls]
            cand_ref[0:PEER_TOPK, ls] = tv_ref[0, 0:1, ls] + tv1
            for a in range(1, 8):
                row = 16 + 8 * (a - 1)
                cand_ref[row:row + 8, ls] = jnp.where(sub8 < PEER_NB[a], tv_ref[0, a:a + 1, ls] + tv1[0:8, :],
                                                      -jnp.inf)
            cand_ref[72:80, ls] = tv_ref[0, 8:PEER_TOPK, ls] + tv1[0:1, :]

            def write2(r, m, idx, ls=ls):
                bv_ref[r:r + 1, ls] = m
                bp_ref[r:r + 1, ls] = idx

            _topk_rows(cand_ref[:, ls], cand_codes, PEER_TOPK, write2)
        best = bv_ref[...]
        pos = bp_ref[...].astype(jnp.int32)
        a_sel = pos >> 4
        b_sel = pos & (PEER_TOPK - 1)
        isel = jnp.zeros_like(best)
        jsel = jnp.zeros_like(best)
        for a in range(PEER_TOPK):
            isel = jnp.where(a_sel == a, ti_ref[0, a:a + 1, :], isel)
            jsel = jnp.where(b_sel == a, ti_ref[1, a:a + 1, :], jsel)
        ex = jnp.exp(best - best[0:1, :])
        wgt = ex / jnp.sum(ex, axis=0, keepdims=True)
        hs = pl.ds(pl.multiple_of(head * PEER_TOPK, PEER_TOPK), PEER_TOPK)
        oi_ref[hs, :] = isel
        oj_ref[hs, :] = jsel
        ow_ref[hs, :] = wgt


def _peer_kernel(hn_e_ref, hn_s_ref, wq_ref, keys_ref, h_ref, u_ref, v_ref, lnw_ref, o_ref,
                 r_ref, tv_ref, ti_ref, cand_ref, bv_ref, bp_ref, oi_ref, oj_ref, ow_ref, it_ref, jt_ref, wt_ref,
                 *, tn, pitch):
    i = pl.program_id(0)
    e = pl.program_id(1)
    ipb = PEER_EB // PEER_N_KEYS

    @pl.when((i == 0) & (e == 0))
    def _():
        def zero_slab(k, carry):
            r_ref[pl.ds(pl.multiple_of(k * pitch, 8), pitch), :] = jnp.zeros((pitch, PEER_N_KEYS), F32)
            return carry

        lax.fori_loop(0, PEER_N_KEYS, zero_slab, 0)

    @pl.when((i > 0) & (e == 0))
    def _():
        it_ref[...] = oi_ref[...].T
        jt_ref[...] = oj_ref[...].T
        wt_ref[...] = ow_ref[...].T
        sub = lax.broadcasted_iota(jnp.int32, (PEER_N_KEYS, PEER_SLOTS), 0).astype(F32)

        def body(n, carry):
            irow = it_ref[pl.ds(n, 1), :]
            jrow = jt_ref[pl.ds(n, 1), :]
            wrow = wt_ref[pl.ds(n, 1), :]
            w_hi = wrow.astype(BF16)
            w_lo = (wrow - w_hi.astype(F32)).astype(BF16)
            hit = sub == irow
            q_hi = jnp.where(hit, w_hi.astype(F32), 0.0).astype(BF16)
            q_lo = jnp.where(hit, w_lo.astype(F32), 0.0).astype(BF16)
            p_t = jnp.where(sub == jrow, 1.0, 0.0).T.astype(BF16)
            r = (jnp.dot(q_hi, p_t, preferred_element_type=F32)
                 + jnp.dot(q_lo, p_t, preferred_element_type=F32))
            r_ref[pl.ds(n, PEER_N_KEYS, stride=pitch), :] = r
            return carry

        lax.fori_loop(0, tn, body, 0, unroll=32)

    @pl.when(e == 0)
    def _():
        o_ref[...] = jnp.zeros_like(o_ref)

    _select_level1(e, hn_s_ref, wq_ref, keys_ref, tv_ref, ti_ref)

    act = lax.dot_general(hn_e_ref[...], u_ref[...], (((1,), (1,)), ((), ())),
                          preferred_element_type=F32)
    wblk = jnp.concatenate(
        [r_ref[pl.ds(pl.multiple_of((e * ipb + ii) * pitch, 8), tn), :] for ii in range(ipb)], axis=1)
    g = (_gelu_tanh(act) * wblk).astype(BF16)
    o_ref[...] += jnp.dot(g, v_ref[...], preferred_element_type=F32)

    _select_level2(e, tv_ref, ti_ref, cand_ref, bv_ref, bp_ref, oi_ref, oj_ref, ow_ref)

    @pl.when(e == pl.num_programs(1) - 1)
    def _():
        h = h_ref[...] + o_ref[...]
        ms = jnp.mean(h * h, axis=-1, keepdims=True)
        o_ref[...] = h * lax.rsqrt(ms + RMS_EPS) * lnw_ref[...]


def _peer(hn2, h2, wq_t, keys, u_tab, v_tab, lnw):
    t = hn2.shape[0]
    tn = min(PEER_TN, t)
    nt = t // tn
    pitch = tn + 8
    n_exp = u_tab.shape[0]
    assert n_exp // PEER_EB == 2 * PEER_HEADS
    kern = functools.partial(_peer_kernel, tn=tn, pitch=pitch)
    prev = lambda i, e: (jnp.maximum(i - 1, 0), 0)
    return pl.pallas_call(
        kern,
        grid=(nt + 1, n_exp // PEER_EB),
        in_specs=[
            pl.BlockSpec((tn, D_MODEL), prev),
            pl.BlockSpec((tn, D_MODEL), lambda i, e: (jnp.minimum(i, nt - 1), 0)),
            pl.BlockSpec((PEER_HALF, D_MODEL), lambda i, e: (e, 0)),
            pl.BlockSpec((1, 1, PEER_N_KEYS, PEER_HALF), lambda i, e: (e // 2, e % 2, 0, 0)),
            pl.BlockSpec((tn, D_MODEL), prev),
            pl.BlockSpec((PEER_EB, D_MODEL), lambda i, e: (e, 0)),
            pl.BlockSpec((PEER_EB, D_MODEL), lambda i, e: (e, 0)),
            pl.BlockSpec((1, D_MODEL), lambda i, e: (0, 0)),
        ],
        out_specs=pl.BlockSpec((tn, D_MODEL), prev),
        out_shape=jax.ShapeDtypeStruct((t, D_MODEL), F32),
        scratch_shapes=[
            pltpu.VMEM((PEER_N_KEYS * pitch, PEER_N_KEYS), F32),
            pltpu.VMEM((2, PEER_TOPK, tn), F32),
            pltpu.VMEM((2, PEER_TOPK, tn), F32),
            pltpu.VMEM((PEER_CAND_ROWS, tn), F32),
            pltpu.VMEM((PEER_TOPK, tn), F32),
            pltpu.VMEM((PEER_TOPK, tn), F32),
            pltpu.VMEM((PEER_SLOTS, tn), F32),
            pltpu.VMEM((PEER_SLOTS, tn), F32),
            pltpu.VMEM((PEER_SLOTS, tn), F32),
            pltpu.VMEM((tn, PEER_SLOTS), F32),
            pltpu.VMEM((tn, PEER_SLOTS), F32),
            pltpu.VMEM((tn, PEER_SLOTS), F32),
        ],
        compiler_params=pltpu.CompilerParams(
            dimension_semantics=("arbitrary", "arbitrary"), vmem_limit_bytes=V7X_VMEM_LIMIT_EXPERTS),
        name="peer",
    )(hn2, hn2, wq_t, keys, h2, u_tab, v_tab, lnw)


def _constants():
    t = jnp.arange(CHUNK)
    tri = (t[None, :] <= t[:, None]).astype(BF16)
    lane = jnp.arange(SSD_D_INNER)
    small_lane = jnp.arange(SMALL_W)
    head_of_lane = small_lane - SMALL_DT
    is_head = (head_of_lane >= 0) & (head_of_lane < SSD_HEADS)
    expand = (small_lane[:, None] == SMALL_DT + lane[None, :] // SSD_HEADDIM).astype(BF16)
    psel = (is_head[None, :] & (head_of_lane[None, :] // 2 == jnp.arange(SSD_HEADS // 2)[:, None])).astype(BF16)
    par = (is_head[None, :] & (head_of_lane[None, :] % 2 == jnp.arange(2)[:, None])).astype(BF16)
    gl = jnp.arange(SSD_GW)
    bmask = (gl[:, None] // CHUNK == gl[None, :] // SSD_HEADDIM).astype(BF16)
    return tri, expand, psel, par, bmask


def _pack_in_weights(w_in):
    o = 0
    parts = {}
    for name, width in (("q", GLA_QK), ("k", GLA_QK), ("v", GLA_VW), ("gout", GLA_VW), ("glow", GLA_GATE_RANK),
                        ("z", SSD_D_INNER), ("xbc", SSD_XBC), ("dt", SSD_HEADS), ("ga", D_MODEL), ("gb", D_MODEL)):
        parts[name] = w_in[:, o:o + width]
        o += width
    main = jnp.concatenate([parts[n] for n in ("q", "k", "v", "z", "gout", "ga", "xbc", "gb")], axis=1)
    small = jnp.concatenate(
        [parts["glow"], parts["dt"], jnp.zeros((D_MODEL, SMALL_W - GLA_GATE_RANK - SSD_HEADS), w_in.dtype)], axis=1)
    return main.astype(BF16), small.astype(BF16)


def _pad_small(vec, offset):
    out = jnp.zeros((1, SMALL_W), F32)
    return out.at[0, offset:offset + vec.shape[0]].set(vec.astype(F32))


def kernel(x, meta_tokens, ln_mix_w, w_in, gla_w_gate2, gla_b_gate, gla_norm_w, ssd_conv_w, ssd_conv_b,
           ssd_dt_bias, ssd_a_log, ssd_d, ssd_norm_w, w_up_gla, w_up_ssd, w_out, ln_ffn_w,
           peer_w_q, peer_sub_keys, peer_u, peer_v, ln_final_w):
    bsz, seq, _ = x.shape
    assert seq % (4 * CHUNK) == 0
    l = 0
    consts = _constants()
    tri = consts[0]

    w_main, w_small = _pack_in_weights(w_in[l])
    lnw = ln_mix_w[l].reshape(1, D_MODEL)
    w2p = jnp.zeros((SMALL_W, GLA_QK), F32).at[SMALL_GLOW:SMALL_GLOW + GLA_GATE_RANK].set(gla_w_gate2[l]).astype(BF16)
    bgate = gla_b_gate[l].reshape(1, GLA_QK)
    gnw = gla_norm_w[l].reshape(1, GLA_DV)
    cw = ssd_conv_w[l]
    cbias = ssd_conv_b[l].reshape(1, SSD_XBC)
    dtb = _pad_small(ssd_dt_bias[l], SMALL_DT)
    alog = _pad_small(ssd_a_log[l], SMALL_DT)
    dsk = jnp.repeat(ssd_d[l].astype(F32), SSD_HEADDIM).reshape(1, SSD_D_INNER)
    snw = ssd_norm_w[l].reshape(1, SSD_D_INNER)

    meta_rows = jnp.concatenate([jnp.zeros((META_PAD, D_MODEL), F32), meta_tokens.astype(F32)], axis=0)
    proj_m, small_m = _inproj(meta_rows, lnw, w_main, w_small)
    gla_s0 = jnp.zeros((GLA_HEADS, GLA_DV, GLA_DK), F32)
    ssd_s0 = jnp.zeros((SSD_GROUPS, SSD_STATE, SSD_GW), F32)
    u0 = jnp.zeros((8, SSD_XBC), F32)
    _, gla_s1 = _gla(proj_m, small_m, w2p, bgate, gnw, gla_s0, tri, batch=1, seq=CHUNK, cb=1, mask_rows=META_PAD)
    _, ssd_s1, u1 = _ssd(proj_m, small_m, cw, cbias, dtb, alog, dsk, snw, ssd_s0, u0, consts,
                         batch=1, seq=CHUNK, cb=1, mask_rows=META_PAD)

    x2d = x.reshape(bsz * seq, D_MODEL)
    proj, small = _inproj(x2d, lnw, w_main, w_small)
    og, _ = _gla(proj, small, w2p, bgate, gnw, gla_s1[0], tri, batch=bsz, seq=seq, cb=4, mask_rows=0)
    yg, _, _ = _ssd(proj, small, cw, cbias, dtb, alog, dsk, snw, ssd_s1[0], u1[0], consts,
                    batch=bsz, seq=seq, cb=4, mask_rows=0)
    h2, hn2 = _merge(og, yg, proj, x2d, w_up_gla[l].astype(BF16), w_up_ssd[l].astype(BF16),
                     w_out[l].astype(BF16), ln_ffn_w[l].reshape(1, D_MODEL))

    wq_t = peer_w_q[l].T.astype(BF16)
    keys = peer_sub_keys[l].astype(BF16)
    out = _peer(hn2, h2, wq_t, keys, peer_u[l].astype(BF16), peer_v[l].astype(BF16),
                ln_final_w.reshape(1, D_MODEL))
    return out.reshape(bsz, seq, D_MODEL)
```

```python
import functools
import math

import jax
import jax.numpy as jnp
from jax import lax
from jax.experimental import pallas as pl
from jax.experimental.pallas import tpu as pltpu

F32 = jnp.float32
BF16 = jnp.bfloat16

D_MODEL = 1024
CHUNK = 64
N_META = 16
META_PAD = CHUNK - N_META
RMS_EPS = 1e-6
GLA_HEADS = 4
GLA_DK = 128
GLA_DV = 256
GLA_GATE_RANK = 16
GLA_GATE_NORM = 16.0
GLA_QK = 512
GLA_VW = 1024
SSD_D_INNER = 2048
SSD_HEADDIM = 64
SSD_HEADS = 32
SSD_GROUPS = 4
SSD_HPG = 8
SSD_STATE = 128
SSD_CONV = 4
SSD_BC = 512
SSD_XBC = 3072
SSD_GW = SSD_D_INNER // SSD_GROUPS
PEER_HEADS = 8
PEER_N_KEYS = 128
PEER_HALF = 128
PEER_TOPK = 16
PEER_SLOTS = PEER_HEADS * PEER_TOPK

COL_Q, COL_K, COL_V, COL_Z, COL_GOUT, COL_GA, COL_XBC, COL_GB = 0, 512, 1024, 2048, 4096, 5120, 6144, 9216
MAIN_W = 10240
SMALL_W = 128
SMALL_GLOW = 0
SMALL_DT = 16

V7X_VMEM_LIMIT = 56 * 1024 * 1024
V7X_VMEM_LIMIT_EXPERTS = 60 * 1024 * 1024


def _dot(a, b):
    return jnp.dot(a.astype(BF16), b.astype(BF16), preferred_element_type=F32)


def _dot_nt(a, b):
    return lax.dot_general(a.astype(BF16), b.astype(BF16), (((1,), (1,)), ((), ())),
                           preferred_element_type=F32)


def _dot_tn(a, b):
    return lax.dot_general(a.astype(BF16), b.astype(BF16), (((0,), (0,)), ((), ())),
                           preferred_element_type=F32)


def _split3(x):
    x1 = x.astype(BF16)
    r = x - x1.astype(F32)
    x2 = r.astype(BF16)
    x3 = (r - x2.astype(F32)).astype(BF16)
    return x1, x2, x3


def _dot_exact_l(m, x):
    x1, x2, x3 = _split3(x)
    return (jnp.dot(m, x1, preferred_element_type=F32) + jnp.dot(m, x2, preferred_element_type=F32)
            + jnp.dot(m, x3, preferred_element_type=F32))


def _sigmoid(x):
    return 0.5 * jnp.tanh(0.5 * x) + 0.5


def _silu(x):
    return x * _sigmoid(x)


def _softplus(x):
    return jnp.maximum(x, 0.0) + jnp.log1p(jnp.exp(-jnp.abs(x)))


def _inproj_kernel(x_ref, lnw_ref, w_ref, ws_ref, o_ref, os_ref, hn_ref):
    @pl.when(pl.program_id(1) == 0)
    def _():
        x = x_ref[...]
        ms = jnp.mean(x * x, axis=-1, keepdims=True)
        hn_ref[...] = (x * lax.rsqrt(ms + RMS_EPS) * lnw_ref[...]).astype(BF16)
        os_ref[...] = jnp.dot(hn_ref[...], ws_ref[...], preferred_element_type=F32)

    o_ref[...] = jnp.dot(hn_ref[...], w_ref[...], preferred_element_type=F32)


def _inproj(x2d, lnw, w_main, w_small):
    rows = x2d.shape[0]
    tm = min(1024, rows)
    tn = 2560
    return pl.pallas_call(
        _inproj_kernel,
        grid=(rows // tm, MAIN_W // tn),
        in_specs=[
            pl.BlockSpec((tm, D_MODEL), lambda i, j: (i, 0)),
            pl.BlockSpec((1, D_MODEL), lambda i, j: (0, 0)),
            pl.BlockSpec((D_MODEL, tn), lambda i, j: (0, j)),
            pl.BlockSpec((D_MODEL, SMALL_W), lambda i, j: (0, 0)),
        ],
        out_specs=[
            pl.BlockSpec((tm, tn), lambda i, j: (i, j)),
            pl.BlockSpec((tm, SMALL_W), lambda i, j: (i, 0)),
        ],
        out_shape=[
            jax.ShapeDtypeStruct((rows, MAIN_W), F32),
            jax.ShapeDtypeStruct((rows, SMALL_W), F32),
        ],
        scratch_shapes=[pltpu.VMEM((tm, D_MODEL), BF16)],
        compiler_params=pltpu.CompilerParams(
            dimension_semantics=("arbitrary", "arbitrary"), vmem_limit_bytes=V7X_VMEM_LIMIT),
        name="inproj",
    )(x2d, lnw, w_main, w_small)


def _gla_kernel(q_ref, k_ref, v_ref, g_ref, sm_ref, w2_ref, bg_ref, nw_ref, s0_ref, tri_ref,
                og_ref, sfin_ref, s_ref, *, cb, mask_rows):
    c = pl.program_id(1)

    @pl.when(c == 0)
    def _():
        s_ref[...] = s0_ref[...]

    tri = tri_ref[...]
    tri_b = (lax.broadcasted_iota(jnp.int32, (CHUNK, CHUNK), 1)
             <= lax.broadcasted_iota(jnp.int32, (CHUNK, CHUNK), 0))
    scale = GLA_DK ** -0.5

    def chunk_step(ci, carry):
        rs = pl.ds(pl.multiple_of(ci * CHUNK, CHUNK), CHUNK)
        lg = _dot(sm_ref[rs, :], w2_ref[...]) + bg_ref[...]
        lg = (jnp.minimum(lg, 0.0) - jnp.log1p(jnp.exp(-jnp.abs(lg)))) * (1.0 / GLA_GATE_NORM)
        if mask_rows:
            keep = lax.broadcasted_iota(jnp.int32, (CHUNK, GLA_QK), 0) >= mask_rows
            lg = jnp.where(keep, lg, 0.0)
        gcum = _dot_exact_l(tri, lg)
        for h in range(GLA_HEADS):
            ks = slice(h * GLA_DK, (h + 1) * GLA_DK)
            vs = slice(h * GLA_DV, (h + 1) * GLA_DV)
            qh = q_ref[rs, ks] * scale
            kh = k_ref[rs, ks]
            vh = v_ref[rs, vs]
            if mask_rows:
                keep_k = lax.broadcasted_iota(jnp.int32, (CHUNK, GLA_DK), 0) >= mask_rows
                keep_v = lax.broadcasted_iota(jnp.int32, (CHUNK, GLA_DV), 0) >= mask_rows
                qh = jnp.where(keep_k, qh, 0.0)
                kh = jnp.where(keep_k, kh, 0.0)
                vh = jnp.where(keep_v, vh, 0.0)
            gh = gcum[:, ks]
            eg = jnp.exp(gh)
            egn = jnp.exp(-gh)
            qe = qh * eg
            a_causal = _dot_nt(qe, kh * egn)
            a_ahead = _dot_nt(qh * egn, kh * eg)
            att = jnp.where(tri_b, a_causal, a_ahead)
            st = s_ref[h]
            o = _dot(att, vh) + _dot_nt(qe, st)
            glast = gh[CHUNK - 1:CHUNK, :]
            kd = kh * jnp.exp(glast - gh)
            s_ref[h] = st * jnp.exp(glast) + _dot_tn(vh, kd)
            ms = jnp.mean(o * o, axis=-1, keepdims=True)
            on = o * lax.rsqrt(ms + RMS_EPS) * nw_ref[...]
            og_ref[rs, vs] = (on * _silu(g_ref[rs, vs])).astype(og_ref.dtype)
        return carry

    lax.fori_loop(0, cb, chunk_step, 0, unroll=2)

    @pl.when(c == pl.num_programs(1) - 1)
    def _():
        sfin_ref[0] = s_ref[...]


def _gla(proj, small, w2p, bgate, normw, s0, tri, *, batch, seq, cb, mask_rows):
    nb = seq // (cb * CHUNK)
    rb = cb * CHUNK
    row = lambda b, c: b * nb + c
    kern = functools.partial(_gla_kernel, cb=cb, mask_rows=mask_rows)
    return pl.pallas_call(
        kern,
        grid=(batch, nb),
        in_specs=[
            pl.BlockSpec((rb, GLA_QK), lambda b, c: (row(b, c), COL_Q // GLA_QK)),
            pl.BlockSpec((rb, GLA_QK), lambda b, c: (row(b, c), COL_K // GLA_QK)),
            pl.BlockSpec((rb, GLA_VW), lambda b, c: (row(b, c), COL_V // GLA_VW)),
            pl.BlockSpec((rb, GLA_VW), lambda b, c: (row(b, c), COL_GOUT // GLA_VW)),
            pl.BlockSpec((rb, SMALL_W), lambda b, c: (row(b, c), 0)),
            pl.BlockSpec((SMALL_W, GLA_QK), lambda b, c: (0, 0)),
            pl.BlockSpec((1, GLA_QK), lambda b, c: (0, 0)),
            pl.BlockSpec((1, GLA_DV), lambda b, c: (0, 0)),
            pl.BlockSpec((GLA_HEADS, GLA_DV, GLA_DK), lambda b, c: (0, 0, 0)),
            pl.BlockSpec((CHUNK, CHUNK), lambda b, c: (0, 0)),
        ],
        out_specs=[
            pl.BlockSpec((rb, GLA_VW), lambda b, c: (row(b, c), 0)),
            pl.BlockSpec((1, GLA_HEADS, GLA_DV, GLA_DK), lambda b, c: (b, 0, 0, 0)),
        ],
        out_shape=[
            jax.ShapeDtypeStruct((batch * seq, GLA_VW), BF16),
            jax.ShapeDtypeStruct((batch, GLA_HEADS, GLA_DV, GLA_DK), F32),
        ],
        scratch_shapes=[pltpu.VMEM((GLA_HEADS, GLA_DV, GLA_DK), F32)],
        compiler_params=pltpu.CompilerParams(
            dimension_semantics=("arbitrary", "arbitrary"), vmem_limit_bytes=V7X_VMEM_LIMIT),
        name="gla_scan",
    )(proj, proj, proj, proj, small, w2p, bgate, normw, s0, tri)


def _ssd_kernel(xbc_ref, z_ref, sm_ref, cw_ref, cbias_ref, dtb_ref, alog_ref, dsk_ref, nw_ref,
                s0_ref, u0_ref, tri_ref, exp_ref, psel_ref, par_ref, bmask_ref,
                yg_ref, sfin_ref, ufin_ref, s_ref, ext_ref, xc_ref, *, cb, mask_rows):
    c = pl.program_id(1)

    @pl.when(c == 0)
    def _():
        s_ref[...] = s0_ref[...]
        ext_ref[0:8, :] = u0_ref[...]

    tri = tri_ref[...]
    psel = psel_ref[...]
    par0 = par_ref[0:1, :]
    par1 = par_ref[1:2, :]
    lane_blk = 512

    def chunk_step(ci, carry):
        rs = pl.ds(pl.multiple_of(ci * CHUNK, CHUNK), CHUNK)
        for lb in range(SSD_XBC // lane_blk):
            ls = slice(lb * lane_blk, (lb + 1) * lane_blk)
            ext_ref[8:8 + CHUNK, ls] = xbc_ref[rs, ls]
            conv = cbias_ref[:, ls] + ext_ref[8:8 + CHUNK, ls] * cw_ref[SSD_CONV - 1:SSD_CONV, ls]
            for j in range(1, SSD_CONV):
                conv = conv + ext_ref[8 - j:8 - j + CHUNK, ls] * cw_ref[SSD_CONV - 1 - j:SSD_CONV - j, ls]
            xc = _silu(conv)
            if mask_rows:
                xc = jnp.where(lax.broadcasted_iota(jnp.int32, (CHUNK, lane_blk), 0) >= mask_rows, xc, 0.0)
            xc_ref[:, ls] = xc
            ext_ref[0:8, ls] = ext_ref[CHUNK:CHUNK + 8, ls]
        dt = _softplus(sm_ref[rs, :] + dtb_ref[...])
        if mask_rows:
            dt = jnp.where(lax.broadcasted_iota(jnp.int32, (CHUNK, SMALL_W), 0) >= mask_rows, dt, 0.0)
        a = -dt * jnp.exp(alog_ref[...])
        acum = _dot_exact_l(tri, a)
        acum_parts = _split3(acum)
        parts6 = jnp.concatenate(acum_parts + _split3(dt), axis=0)
        acum_t = sum(
            lax.dot_general(psel, jnp.concatenate([x * par0, x * par1], axis=0), (((1,), (1,)), ((), ())),
                            preferred_element_type=F32) for x in acum_parts)
        for g in range(SSD_GROUPS):
            gs = slice(g * SSD_GW, (g + 1) * SSD_GW)
            expand = exp_ref[:, gs]
            ex = jnp.dot(parts6, expand, preferred_element_type=F32)
            colb = ex[0:CHUNK] + ex[CHUNK:2 * CHUNK] + ex[2 * CHUNK:3 * CHUNK]
            dtx = ex[3 * CHUNK:4 * CHUNK] + ex[4 * CHUNK:5 * CHUNK] + ex[5 * CHUNK:]
            rowb = jnp.concatenate(
                [jnp.broadcast_to(acum_t[4 * g + k:4 * g + k + 1, :], (CHUNK, 2 * CHUNK)) for k in range(4)],
                axis=1)
            seg = jnp.exp(-jnp.abs(colb - rowb))
            clast = colb[CHUNK - 1:CHUNK, :]
            xs = xc_ref[:, gs]
            bg = xc_ref[:, SSD_D_INNER + g * SSD_STATE:SSD_D_INNER + (g + 1) * SSD_STATE]
            cg = xc_ref[:, SSD_D_INNER + SSD_BC + g * SSD_STATE:SSD_D_INNER + SSD_BC + (g + 1) * SSD_STATE]
            xg = xs * dtx
            cbx = _dot_nt(cg, jnp.concatenate([bg] * SSD_HPG, axis=0))
            sc = seg * cbx
            bd = jnp.concatenate([xg.astype(BF16)] * SSD_HPG, axis=0) * bmask_ref[...]
            st = s_ref[g]
            y = _dot(sc, bd) + _dot(cg, st) * jnp.exp(colb)
            s_ref[g] = st * jnp.exp(clast) + _dot_tn(bg, xg * jnp.exp(clast - colb))
            y = y + dsk_ref[:, gs] * xs
            y = y * _silu(z_ref[rs, gs])
            ms = jnp.mean(y * y, axis=-1, keepdims=True)
            yg_ref[rs, gs] = (y * lax.rsqrt(ms + RMS_EPS) * nw_ref[:, gs]).astype(yg_ref.dtype)
        return carry

    lax.fori_loop(0, cb, chunk_step, 0, unroll=2)

    @pl.when(c == pl.num_programs(1) - 1)
    def _():
        sfin_ref[0] = s_ref[...]
        ufin_ref[0] = ext_ref[0:8, :]


def _ssd(proj, small, cw, cbias, dtb, alog, dsk, normw, s0, u0, consts, *, batch, seq, cb, mask_rows):
    nb = seq // (cb * CHUNK)
    rb = cb * CHUNK
    row = lambda b, c: b * nb + c
    tri, expand, psel, par, bmask = consts
    kern = functools.partial(_ssd_kernel, cb=cb, mask_rows=mask_rows)
    full2 = lambda shape: pl.BlockSpec(shape, lambda b, c: (0, 0))
    return pl.pallas_call(
        kern,
        grid=(batch, nb),
        in_specs=[
            pl.BlockSpec((rb, SSD_XBC), lambda b, c: (row(b, c), COL_XBC // SSD_XBC)),
            pl.BlockSpec((rb, SSD_D_INNER), lambda b, c: (row(b, c), COL_Z // SSD_D_INNER)),
            pl.BlockSpec((rb, SMALL_W), lambda b, c: (row(b, c), 0)),
            full2((SSD_CONV, SSD_XBC)),
            full2((1, SSD_XBC)),
            full2((1, SMALL_W)),
            full2((1, SMALL_W)),
            full2((1, SSD_D_INNER)),
            full2((1, SSD_D_INNER)),
            pl.BlockSpec((SSD_GROUPS, SSD_STATE, SSD_GW), lambda b, c: (0, 0, 0)),
            full2((8, SSD_XBC)),
            full2((CHUNK, CHUNK)),
            full2((SMALL_W, SSD_D_INNER)),
            full2((SSD_HEADS // 2, SMALL_W)),
            full2((2, SMALL_W)),
            full2((SSD_GW, SSD_GW)),
        ],
        out_specs=[
            pl.BlockSpec((rb, SSD_D_INNER), lambda b, c: (row(b, c), 0)),
            pl.BlockSpec((1, SSD_GROUPS, SSD_STATE, SSD_GW), lambda b, c: (b, 0, 0, 0)),
            pl.BlockSpec((1, 8, SSD_XBC), lambda b, c: (b, 0, 0)),
        ],
        out_shape=[
            jax.ShapeDtypeStruct((batch * seq, SSD_D_INNER), BF16),
            jax.ShapeDtypeStruct((batch, SSD_GROUPS, SSD_STATE, SSD_GW), F32),
            jax.ShapeDtypeStruct((batch, 8, SSD_XBC), F32),
        ],
        scratch_shapes=[pltpu.VMEM((SSD_GROUPS, SSD_STATE, SSD_GW), F32),
                        pltpu.VMEM((CHUNK + 8, SSD_XBC), F32),
                        pltpu.VMEM((CHUNK, SSD_XBC), F32)],
        compiler_params=pltpu.CompilerParams(
            dimension_semantics=("arbitrary", "arbitrary"), vmem_limit_bytes=V7X_VMEM_LIMIT),
        name="ssd_scan",
    )(proj, proj, small, cw, cbias, dtb, alog, dsk, normw, s0, u0, tri, expand, psel, par, bmask)


def _merge_kernel(og_ref, yg_ref, ga_ref, gb_ref, x_ref, wa_ref, wb_ref, wo_ref, lnw_ref, h_ref, hn_ref):
    ya = jnp.dot(og_ref[...], wa_ref[...], preferred_element_type=F32)
    yb = jnp.dot(yg_ref[...], wb_ref[...], preferred_element_type=F32)
    mixed = _sigmoid(ga_ref[...]) * ya + _sigmoid(gb_ref[...]) * yb
    h = x_ref[...] + _dot(mixed, wo_ref[...])
    h_ref[...] = h
    ms = jnp.mean(h * h, axis=-1, keepdims=True)
    hn_ref[...] = (h * lax.rsqrt(ms + RMS_EPS) * lnw_ref[...]).astype(BF16)


def _merge(og, yg, proj, x2d, wa, wb, wo, lnw):
    rows = x2d.shape[0]
    tm = min(512, rows)
    full = lambda shape: pl.BlockSpec(shape, lambda i: (0, 0))
    return pl.pallas_call(
        _merge_kernel,
        grid=(rows // tm,),
        in_specs=[
            pl.BlockSpec((tm, GLA_VW), lambda i: (i, 0)),
            pl.BlockSpec((tm, SSD_D_INNER), lambda i: (i, 0)),
            pl.BlockSpec((tm, D_MODEL), lambda i: (i, COL_GA // D_MODEL)),
            pl.BlockSpec((tm, D_MODEL), lambda i: (i, COL_GB // D_MODEL)),
            pl.BlockSpec((tm, D_MODEL), lambda i: (i, 0)),
            full((GLA_VW, D_MODEL)),
            full((SSD_D_INNER, D_MODEL)),
            full((D_MODEL, D_MODEL)),
            full((1, D_MODEL)),
        ],
        out_specs=[
            pl.BlockSpec((tm, D_MODEL), lambda i: (i, 0)),
            pl.BlockSpec((tm, D_MODEL), lambda i: (i, 0)),
        ],
        out_shape=[
            jax.ShapeDtypeStruct((rows, D_MODEL), F32),
            jax.ShapeDtypeStruct((rows, D_MODEL), BF16),
        ],
        compiler_params=pltpu.CompilerParams(
            dimension_semantics=("arbitrary",), vmem_limit_bytes=V7X_VMEM_LIMIT),
        name="merge",
    )(og, yg, proj, proj, x2d, wa, wb, wo, lnw)


PEER_TN = 512
PEER_EB = 1024
PEER_CW = 128


def _topk_rows(vals, codes, k, write):
    n = vals.shape[0]
    tiles = [vals[t:t + 8] for t in range(0, n, 8)]
    ctiles = [codes[t:t + 8] for t in range(0, n, 8)]
    none8 = jnp.full((8, vals.shape[1]), _CODE_NONE, F32)

    def lowest_hit(ts, cs, m):
        c8 = none8
        for t, c in zip(ts[::-1], cs[::-1]):
            c8 = jnp.where(t == m, c, c8)
        return c8

    for r in range(k):
        level = tiles
        while len(level) > 1:
            level = [jnp.maximum(level[j], level[j + 1]) if j + 1 < len(level) else level[j]
                     for j in range(0, len(level), 2)]
        m = jnp.max(level[0], axis=0, keepdims=True)
        mid = len(tiles) // 2
        c8 = jnp.minimum(lowest_hit(tiles[:mid], ctiles[:mid], m), lowest_hit(tiles[mid:], ctiles[mid:], m))
        idx = jnp.min(c8, axis=0, keepdims=True)
        write(r, m, idx)
        tiles = [jnp.where(c == idx, -jnp.inf, t) for t, c in zip(tiles, ctiles)]


_CODE_NONE = float(1 << 20)
PEER_NB = (16, 8, 5, 4, 3, 2, 2, 2)
PEER_CAND_ROWS = 16 + 8 * 7 + 8


def _cand_codes(lanes):
    r = lax.broadcasted_iota(jnp.int32, (PEER_CAND_ROWS, lanes), 0)
    mid = r - 16
    code_mid = (((mid >> 3) + 1) << 4) + (mid & 7)
    code_last = (r - 72 + 8) << 4
    return jnp.where(r < 16, r, jnp.where(r < 72, code_mid, code_last)).astype(F32)


_GELU_C = math.sqrt(2.0 / math.pi)


def _gelu_tanh_times(x, half_w):
    inner = _GELU_C * (x + 0.044715 * (x * x * x))
    return (x * (1.0 + jnp.tanh(inner))) * half_w


def _select_scores(hn_ref, wq_ref, keys_ref):
    qt = lax.dot_general(wq_ref[...], hn_ref[...], (((1,), (1,)), ((), ())),
                         preferred_element_type=F32)
    return jnp.dot(keys_ref[0, 0], qt.astype(BF16), preferred_element_type=F32)


def _select_level1(e, s, tv_ref, ti_ref):
    half = e % 2
    key_codes = lax.broadcasted_iota(jnp.int32, (PEER_N_KEYS, PEER_CW), 0).astype(F32)
    for lb in range(s.shape[1] // PEER_CW):
        ls = slice(lb * PEER_CW, (lb + 1) * PEER_CW)

        def write1(r, m, idx, ls=ls):
            tv_ref[half, r:r + 1, ls] = m
            ti_ref[half, r:r + 1, ls] = idx

        _topk_rows(s[:, ls], key_codes, PEER_TOPK, write1)


def _select_level2(e, tv_ref, ti_ref, cand_ref, bv_ref, bp_ref, oi_ref, oj_ref, ow_ref):
    head = e // 2
    lanes = tv_ref.shape[2]

    @pl.when(e % 2 == 1)
    def _():
        cand_codes = _cand_codes(PEER_CW)
        sub8 = lax.broadcasted_iota(jnp.int32, (8, PEER_CW), 0)
        for lb in range(lanes // PEER_CW):
            ls = slice(lb * PEER_CW, (lb + 1) * PEER_CW)
            tv1 = tv_ref[1, :, ls]
            cand_ref[0:PEER_TOPK, ls] = tv_ref[0, 0:1, ls] + tv1
            for a in range(1, 8):
                row = 16 + 8 * (a - 1)
                cand_ref[row:row + 8, ls] = jnp.where(sub8 < PEER_NB[a], tv_ref[0, a:a + 1, ls] + tv1[0:8, :],
                                                      -jnp.inf)
            cand_ref[72:80, ls] = tv_ref[0, 8:PEER_TOPK, ls] + tv1[0:1, :]

            def write2(r, m, idx, ls=ls):
                bv_ref[r:r + 1, ls] = m
                bp_ref[r:r + 1, ls] = idx

            _topk_rows(cand_ref[:, ls], cand_codes, PEER_TOPK, write2)
        best = bv_ref[...]
        pos = bp_ref[...].astype(jnp.int32)
        a_sel = pos >> 4
        b_sel = pos & (PEER_TOPK - 1)
        isel = jnp.zeros_like(best)
        jsel = jnp.zeros_like(best)
        for a in range(PEER_TOPK):
            isel = jnp.where(a_sel == a, ti_ref[0, a:a + 1, :], isel)
            jsel = jnp.where(b_sel == a, ti_ref[1, a:a + 1, :], jsel)
        ex = jnp.exp(best - best[0:1, :])
        wgt = ex / jnp.sum(ex, axis=0, keepdims=True)
        hs = pl.ds(pl.multiple_of(head * PEER_TOPK, PEER_TOPK), PEER_TOPK)
        oi_ref[hs, :] = isel
        oj_ref[hs, :] = jsel
        ow_ref[hs, :] = wgt


def _peer_kernel(hn_e_ref, hn_s_ref, wq_ref, keys_ref, h_ref, u_ref, v_ref, lnw_ref, o_ref,
                 r_ref, tv_ref, ti_ref, cand_ref, bv_ref, bp_ref, oi_ref, oj_ref, ow_ref, it_ref, jt_ref, wt_ref,
                 *, tn, pitch):
    i = pl.program_id(0)
    e = pl.program_id(1)
    ipb = PEER_EB // PEER_N_KEYS

    @pl.when((i == 0) & (e == 0))
    def _():
        def zero_slab(k, carry):
            r_ref[pl.ds(pl.multiple_of(k * pitch, 8), pitch), :] = jnp.zeros((pitch, PEER_N_KEYS), F32)
            return carry

        lax.fori_loop(0, PEER_N_KEYS, zero_slab, 0)

    @pl.when((i > 0) & (e == 0))
    def _():
        it_ref[...] = oi_ref[...].T
        jt_ref[...] = oj_ref[...].T
        wt_ref[...] = ow_ref[...].T
        sub = lax.broadcasted_iota(jnp.int32, (PEER_N_KEYS, PEER_SLOTS), 0).astype(F32)

        def body(n, carry):
            irow = it_ref[pl.ds(n, 1), :]
            jrow = jt_ref[pl.ds(n, 1), :]
            wrow = 0.5 * wt_ref[pl.ds(n, 1), :]
            q = jnp.where(sub == irow, wrow, 0.0).astype(BF16)
            p_t = jnp.where(sub == jrow, 1.0, 0.0).T.astype(BF16)
            r = jnp.dot(q, p_t, preferred_element_type=F32)
            r_ref[pl.ds(n, PEER_N_KEYS, stride=pitch), :] = r
            return carry

        lax.fori_loop(0, tn, body, 0, unroll=64)

    @pl.when(e == 0)
    def _():
        o_ref[...] = jnp.zeros_like(o_ref)

    scores = _select_scores(hn_s_ref, wq_ref, keys_ref)

    act = lax.dot_general(hn_e_ref[...], u_ref[...], (((1,), (1,)), ((), ())),
                          preferred_element_type=F32)
    wblk = jnp.concatenate(
        [r_ref[pl.ds(pl.multiple_of((e * ipb + ii) * pitch, 8), tn), :] for ii in range(ipb)], axis=1)
    g = _gelu_tanh_times(act, wblk).astype(BF16)
    o_ref[...] += jnp.dot(g, v_ref[...], preferred_element_type=F32)

    _select_level1(e, scores, tv_ref, ti_ref)
    _select_level2(e, tv_ref, ti_ref, cand_ref, bv_ref, bp_ref, oi_ref, oj_ref, ow_ref)

    @pl.when(e == pl.num_programs(1) - 1)
    def _():
        h = h_ref[...] + o_ref[...]
        ms = jnp.mean(h * h, axis=-1, keepdims=True)
        o_ref[...] = h * lax.rsqrt(ms + RMS_EPS) * lnw_ref[...]


def _peer(hn2, h2, wq_t, keys, u_tab, v_tab, lnw):
    t = hn2.shape[0]
    tn = min(PEER_TN, t)
    nt = t // tn
    pitch = tn + 8
    n_exp = u_tab.shape[0]
    assert n_exp // PEER_EB == 2 * PEER_HEADS
    kern = functools.partial(_peer_kernel, tn=tn, pitch=pitch)
    prev = lambda i, e: (jnp.maximum(i - 1, 0), 0)
    return pl.pallas_call(
        kern,
        grid=(nt + 1, n_exp // PEER_EB),
        in_specs=[
            pl.BlockSpec((tn, D_MODEL), prev),
            pl.BlockSpec((tn, D_MODEL), lambda i, e: (jnp.minimum(i, nt - 1), 0)),
            pl.BlockSpec((PEER_HALF, D_MODEL), lambda i, e: (e, 0)),
            pl.BlockSpec((1, 1, PEER_N_KEYS, PEER_HALF), lambda i, e: (e // 2, e % 2, 0, 0)),
            pl.BlockSpec((tn, D_MODEL), prev),
            pl.BlockSpec((PEER_EB, D_MODEL), lambda i, e: (e, 0)),
            pl.BlockSpec((PEER_EB, D_MODEL), lambda i, e: (e, 0)),
            pl.BlockSpec((1, D_MODEL), lambda i, e: (0, 0)),
        ],
        out_specs=pl.BlockSpec((tn, D_MODEL), prev),
        out_shape=jax.ShapeDtypeStruct((t, D_MODEL), F32),
        scratch_shapes=[
            pltpu.VMEM((PEER_N_KEYS * pitch, PEER_N_KEYS), F32),
            pltpu.VMEM((2, PEER_TOPK, tn), F32),
            pltpu.VMEM((2, PEER_TOPK, tn), F32),
            pltpu.VMEM((PEER_CAND_ROWS, tn), F32),
            pltpu.VMEM((PEER_TOPK, tn), F32),
            pltpu.VMEM((PEER_TOPK, tn), F32),
            pltpu.VMEM((PEER_SLOTS, tn), F32),
            pltpu.VMEM((PEER_SLOTS, tn), F32),
            pltpu.VMEM((PEER_SLOTS, tn), F32),
            pltpu.VMEM((tn, PEER_SLOTS), F32),
            pltpu.VMEM((tn, PEER_SLOTS), F32),
            pltpu.VMEM((tn, PEER_SLOTS), F32),
        ],
        compiler_params=pltpu.CompilerParams(
            dimension_semantics=("arbitrary", "arbitrary"), vmem_limit_bytes=V7X_VMEM_LIMIT_EXPERTS),
        name="peer",
    )(hn2, hn2, wq_t, keys, h2, u_tab, v_tab, lnw)


def _constants():
    t = jnp.arange(CHUNK)
    tri = (t[None, :] <= t[:, None]).astype(BF16)
    lane = jnp.arange(SSD_D_INNER)
    small_lane = jnp.arange(SMALL_W)
    head_of_lane = small_lane - SMALL_DT
    is_head = (head_of_lane >= 0) & (head_of_lane < SSD_HEADS)
    expand = (small_lane[:, None] == SMALL_DT + lane[None, :] // SSD_HEADDIM).astype(BF16)
    psel = (is_head[None, :] & (head_of_lane[None, :] // 2 == jnp.arange(SSD_HEADS // 2)[:, None])).astype(BF16)
    par = (is_head[None, :] & (head_of_lane[None, :] % 2 == jnp.arange(2)[:, None])).astype(BF16)
    gl = jnp.arange(SSD_GW)
    bmask = (gl[:, None] // CHUNK == gl[None, :] // SSD_HEADDIM).astype(BF16)
    return tri, expand, psel, par, bmask


def _pack_in_weights(w_in):
    o = 0
    parts = {}
    for name, width in (("q", GLA_QK), ("k", GLA_QK), ("v", GLA_VW), ("gout", GLA_VW), ("glow", GLA_GATE_RANK),
                        ("z", SSD_D_INNER), ("xbc", SSD_XBC), ("dt", SSD_HEADS), ("ga", D_MODEL), ("gb", D_MODEL)):
        parts[name] = w_in[:, o:o + width]
        o += width
    main = jnp.concatenate([parts[n] for n in ("q", "k", "v", "z", "gout", "ga", "xbc", "gb")], axis=1)
    small = jnp.concatenate(
        [parts["glow"], parts["dt"], jnp.zeros((D_MODEL, SMALL_W - GLA_GATE_RANK - SSD_HEADS), w_in.dtype)], axis=1)
    return main.astype(BF16), small.astype(BF16)


def _pad_small(vec, offset):
    out = jnp.zeros((1, SMALL_W), F32)
    return out.at[0, offset:offset + vec.shape[0]].set(vec.astype(F32))


def kernel(x, meta_tokens, ln_mix_w, w_in, gla_w_gate2, gla_b_gate, gla_norm_w, ssd_conv_w, ssd_conv_b,
           ssd_dt_bias, ssd_a_log, ssd_d, ssd_norm_w, w_up_gla, w_up_ssd, w_out, ln_ffn_w,
           peer_w_q, peer_sub_keys, peer_u, peer_v, ln_final_w):
    bsz, seq, _ = x.shape
    assert seq % (4 * CHUNK) == 0
    l = 0
    consts = _constants()
    tri = consts[0]

    w_main, w_small = _pack_in_weights(w_in[l])
    lnw = ln_mix_w[l].reshape(1, D_MODEL)
    w2p = jnp.zeros((SMALL_W, GLA_QK), F32).at[SMALL_GLOW:SMALL_GLOW + GLA_GATE_RANK].set(gla_w_gate2[l]).astype(BF16)
    bgate = gla_b_gate[l].reshape(1, GLA_QK)
    gnw = gla_norm_w[l].reshape(1, GLA_DV)
    cw = ssd_conv_w[l]
    cbias = ssd_conv_b[l].reshape(1, SSD_XBC)
    dtb = _pad_small(ssd_dt_bias[l], SMALL_DT)
    alog = _pad_small(ssd_a_log[l], SMALL_DT)
    dsk = jnp.repeat(ssd_d[l].astype(F32), SSD_HEADDIM).reshape(1, SSD_D_INNER)
    snw = ssd_norm_w[l].reshape(1, SSD_D_INNER)

    meta_rows = jnp.concatenate([jnp.zeros((META_PAD, D_MODEL), F32), meta_tokens.astype(F32)], axis=0)
    proj_m, small_m = _inproj(meta_rows, lnw, w_main, w_small)
    gla_s0 = jnp.zeros((GLA_HEADS, GLA_DV, GLA_DK), F32)
    ssd_s0 = jnp.zeros((SSD_GROUPS, SSD_STATE, SSD_GW), F32)
    u0 = jnp.zeros((8, SSD_XBC), F32)
    _, gla_s1 = _gla(proj_m, small_m, w2p, bgate, gnw, gla_s0, tri, batch=1, seq=CHUNK, cb=1, mask_rows=META_PAD)
    _, ssd_s1, u1 = _ssd(proj_m, small_m, cw, cbias, dtb, alog, dsk, snw, ssd_s0, u0, consts,
                         batch=1, seq=CHUNK, cb=1, mask_rows=META_PAD)

    x2d = x.reshape(bsz * seq, D_MODEL)
    proj, small = _inproj(x2d, lnw, w_main, w_small)
    og, _ = _gla(proj, small, w2p, bgate, gnw, gla_s1[0], tri, batch=bsz, seq=seq, cb=4, mask_rows=0)
    yg, _, _ = _ssd(proj, small, cw, cbias, dtb, alog, dsk, snw, ssd_s1[0], u1[0], consts,
                    batch=bsz, seq=seq, cb=4, mask_rows=0)
    h2, hn2 = _merge(og, yg, proj, x2d, w_up_gla[l].astype(BF16), w_up_ssd[l].astype(BF16),
                     w_out[l].astype(BF16), ln_ffn_w[l].reshape(1, D_MODEL))

    wq_t = peer_w_q[l].T.astype(BF16)
    keys = peer_sub_keys[l].astype(BF16)
    out = _peer(hn2, h2, wq_t, keys, peer_u[l].astype(BF16), peer_v[l].astype(BF16),
                ln_final_w.reshape(1, D_MODEL))
    return out.reshape(bsz, seq, D_MODEL)
```

```python
import functools
import math

import jax
import jax.numpy as jnp
from jax import lax
from jax.experimental import pallas as pl
from jax.experimental.pallas import tpu as pltpu

F32 = jnp.float32
BF16 = jnp.bfloat16

D_MODEL = 1024
CHUNK = 64
N_META = 16
META_PAD = CHUNK - N_META
RMS_EPS = 1e-6
GLA_HEADS = 4
GLA_DK = 128
GLA_DV = 256
GLA_GATE_RANK = 16
GLA_GATE_NORM = 16.0
GLA_QK = 512
GLA_VW = 1024
SSD_D_INNER = 2048
SSD_HEADDIM = 64
SSD_HEADS = 32
SSD_GROUPS = 4
SSD_HPG = 8
SSD_STATE = 128
SSD_CONV = 4
SSD_BC = 512
SSD_XBC = 3072
SSD_GW = SSD_D_INNER // SSD_GROUPS
PEER_HEADS = 8
PEER_N_KEYS = 128
PEER_HALF = 128
PEER_TOPK = 16
PEER_SLOTS = PEER_HEADS * PEER_TOPK

COL_Q, COL_K, COL_V, COL_Z, COL_GOUT, COL_GA, COL_XBC, COL_GB = 0, 512, 1024, 2048, 4096, 5120, 6144, 9216
MAIN_W = 10240
SMALL_W = 128
SMALL_GLOW = 0
SMALL_DT = 16

V7X_VMEM_LIMIT = 56 * 1024 * 1024
V7X_VMEM_LIMIT_EXPERTS = 60 * 1024 * 1024


def _dot(a, b):
    return jnp.dot(a.astype(BF16), b.astype(BF16), preferred_element_type=F32)


def _dot_nt(a, b):
    return lax.dot_general(a.astype(BF16), b.astype(BF16), (((1,), (1,)), ((), ())),
                           preferred_element_type=F32)


def _dot_tn(a, b):
    return lax.dot_general(a.astype(BF16), b.astype(BF16), (((0,), (0,)), ((), ())),
                           preferred_element_type=F32)


def _split3(x):
    x1 = x.astype(BF16)
    r = x - x1.astype(F32)
    x2 = r.astype(BF16)
    x3 = (r - x2.astype(F32)).astype(BF16)
    return x1, x2, x3


def _dot_exact_l(m, x):
    x1, x2, x3 = _split3(x)
    return (jnp.dot(m, x1, preferred_element_type=F32) + jnp.dot(m, x2, preferred_element_type=F32)
            + jnp.dot(m, x3, preferred_element_type=F32))


def _sigmoid(x):
    return 0.5 * jnp.tanh(0.5 * x) + 0.5


def _silu(x):
    return x * _sigmoid(x)


def _softplus(x):
    return jnp.maximum(x, 0.0) + jnp.log1p(jnp.exp(-jnp.abs(x)))


def _inproj_kernel(x_ref, lnw_ref, w_ref, ws_ref, o_ref, os_ref, hn_ref):
    @pl.when(pl.program_id(1) == 0)
    def _():
        x = x_ref[...]
        ms = jnp.mean(x * x, axis=-1, keepdims=True)
        hn_ref[...] = (x * lax.rsqrt(ms + RMS_EPS) * lnw_ref[...]).astype(BF16)
        os_ref[...] = jnp.dot(hn_ref[...], ws_ref[...], preferred_element_type=F32)

    o_ref[...] = jnp.dot(hn_ref[...], w_ref[...], preferred_element_type=F32)


def _inproj(x2d, lnw, w_main, w_small):
    rows = x2d.shape[0]
    tm = min(1024, rows)
    tn = 2560
    return pl.pallas_call(
        _inproj_kernel,
        grid=(rows // tm, MAIN_W // tn),
        in_specs=[
            pl.BlockSpec((tm, D_MODEL), lambda i, j: (i, 0)),
            pl.BlockSpec((1, D_MODEL), lambda i, j: (0, 0)),
            pl.BlockSpec((D_MODEL, tn), lambda i, j: (0, j)),
            pl.BlockSpec((D_MODEL, SMALL_W), lambda i, j: (0, 0)),
        ],
        out_specs=[
            pl.BlockSpec((tm, tn), lambda i, j: (i, j)),
            pl.BlockSpec((tm, SMALL_W), lambda i, j: (i, 0)),
        ],
        out_shape=[
            jax.ShapeDtypeStruct((rows, MAIN_W), F32),
            jax.ShapeDtypeStruct((rows, SMALL_W), F32),
        ],
        scratch_shapes=[pltpu.VMEM((tm, D_MODEL), BF16)],
        compiler_params=pltpu.CompilerParams(
            dimension_semantics=("arbitrary", "arbitrary"), vmem_limit_bytes=V7X_VMEM_LIMIT),
        name="inproj",
    )(x2d, lnw, w_main, w_small)


def _gla_parts(q_ref, k_ref, v_ref, g_ref, sm_ref, w2_ref, bg_ref, nw_ref, s0_ref, tri_ref,
               og_ref, sfin_ref, s_ref, *, mask_rows):
    def init():
        s_ref[...] = s0_ref[...]

    tri = tri_ref[...]
    tri_b = (lax.broadcasted_iota(jnp.int32, (CHUNK, CHUNK), 1)
             <= lax.broadcasted_iota(jnp.int32, (CHUNK, CHUNK), 0))
    scale = GLA_DK ** -0.5

    def chunk_step(ci):
        rs = pl.ds(pl.multiple_of(ci * CHUNK, CHUNK), CHUNK)
        lg = _dot(sm_ref[rs, :], w2_ref[...]) + bg_ref[...]
        lg = (jnp.minimum(lg, 0.0) - jnp.log1p(jnp.exp(-jnp.abs(lg)))) * (1.0 / GLA_GATE_NORM)
        if mask_rows:
            keep = lax.broadcasted_iota(jnp.int32, (CHUNK, GLA_QK), 0) >= mask_rows
            lg = jnp.where(keep, lg, 0.0)
        gcum = _dot_exact_l(tri, lg)
        for h in range(GLA_HEADS):
            ks = slice(h * GLA_DK, (h + 1) * GLA_DK)
            vs = slice(h * GLA_DV, (h + 1) * GLA_DV)
            qh = q_ref[rs, ks] * scale
            kh = k_ref[rs, ks]
            vh = v_ref[rs, vs]
            if mask_rows:
                keep_k = lax.broadcasted_iota(jnp.int32, (CHUNK, GLA_DK), 0) >= mask_rows
                keep_v = lax.broadcasted_iota(jnp.int32, (CHUNK, GLA_DV), 0) >= mask_rows
                qh = jnp.where(keep_k, qh, 0.0)
                kh = jnp.where(keep_k, kh, 0.0)
                vh = jnp.where(keep_v, vh, 0.0)
            gh = gcum[:, ks]
            eg = jnp.exp(gh)
            egn = jnp.exp(-gh)
            qe = qh * eg
            a_causal = _dot_nt(qe, kh * egn)
            a_ahead = _dot_nt(qh * egn, kh * eg)
            att = jnp.where(tri_b, a_causal, a_ahead)
            st = s_ref[h]
            o = _dot(att, vh) + _dot_nt(qe, st)
            glast = gh[CHUNK - 1:CHUNK, :]
            kd = kh * jnp.exp(glast - gh)
            s_ref[h] = st * jnp.exp(glast) + _dot_tn(vh, kd)
            ms = jnp.mean(o * o, axis=-1, keepdims=True)
            on = o * lax.rsqrt(ms + RMS_EPS) * nw_ref[...]
            og_ref[rs, vs] = (on * _silu(g_ref[rs, vs])).astype(og_ref.dtype)

    def finish():
        sfin_ref[0] = s_ref[...]

    return init, chunk_step, finish


def _ssd_parts(xbc_ref, z_ref, sm_ref, cw_ref, cbias_ref, dtb_ref, alog_ref, dsk_ref, nw_ref,
               s0_ref, u0_ref, tri_ref, exp_ref, psel_ref, par_ref, bmask_ref,
               yg_ref, sfin_ref, ufin_ref, s_ref, ext_ref, xc_ref, *, mask_rows):
    def init():
        s_ref[...] = s0_ref[...]
        ext_ref[0:8, :] = u0_ref[...]

    tri = tri_ref[...]
    psel = psel_ref[...]
    par0 = par_ref[0:1, :]
    par1 = par_ref[1:2, :]
    lane_blk = 512

    def chunk_step(ci):
        rs = pl.ds(pl.multiple_of(ci * CHUNK, CHUNK), CHUNK)
        for lb in range(SSD_XBC // lane_blk):
            ls = slice(lb * lane_blk, (lb + 1) * lane_blk)
            ext_ref[8:8 + CHUNK, ls] = xbc_ref[rs, ls]
            conv = cbias_ref[:, ls] + ext_ref[8:8 + CHUNK, ls] * cw_ref[SSD_CONV - 1:SSD_CONV, ls]
            for j in range(1, SSD_CONV):
                conv = conv + ext_ref[8 - j:8 - j + CHUNK, ls] * cw_ref[SSD_CONV - 1 - j:SSD_CONV - j, ls]
            xc = _silu(conv)
            if mask_rows:
                xc = jnp.where(lax.broadcasted_iota(jnp.int32, (CHUNK, lane_blk), 0) >= mask_rows, xc, 0.0)
            xc_ref[:, ls] = xc
            ext_ref[0:8, ls] = ext_ref[CHUNK:CHUNK + 8, ls]
        dt = _softplus(sm_ref[rs, :] + dtb_ref[...])
        if mask_rows:
            dt = jnp.where(lax.broadcasted_iota(jnp.int32, (CHUNK, SMALL_W), 0) >= mask_rows, dt, 0.0)
        a = -dt * jnp.exp(alog_ref[...])
        acum = _dot_exact_l(tri, a)
        acum_parts = _split3(acum)
        parts6 = jnp.concatenate(acum_parts + _split3(dt), axis=0)
        acum_t = sum(
            lax.dot_general(psel, jnp.concatenate([x * par0, x * par1], axis=0), (((1,), (1,)), ((), ())),
                            preferred_element_type=F32) for x in acum_parts)
        for g in range(SSD_GROUPS):
            gs = slice(g * SSD_GW, (g + 1) * SSD_GW)
            expand = exp_ref[:, gs]
            ex = jnp.dot(parts6, expand, preferred_element_type=F32)
            colb = ex[0:CHUNK] + ex[CHUNK:2 * CHUNK] + ex[2 * CHUNK:3 * CHUNK]
            dtx = ex[3 * CHUNK:4 * CHUNK] + ex[4 * CHUNK:5 * CHUNK] + ex[5 * CHUNK:]
            rowb = jnp.concatenate(
                [jnp.broadcast_to(acum_t[4 * g + k:4 * g + k + 1, :], (CHUNK, 2 * CHUNK)) for k in range(4)],
                axis=1)
            seg = jnp.exp(-jnp.abs(colb - rowb))
            clast = colb[CHUNK - 1:CHUNK, :]
            xs = xc_ref[:, gs]
            bg = xc_ref[:, SSD_D_INNER + g * SSD_STATE:SSD_D_INNER + (g + 1) * SSD_STATE]
            cg = xc_ref[:, SSD_D_INNER + SSD_BC + g * SSD_STATE:SSD_D_INNER + SSD_BC + (g + 1) * SSD_STATE]
            xg = xs * dtx
            cbx = _dot_nt(cg, jnp.concatenate([bg] * SSD_HPG, axis=0))
            sc = seg * cbx
            bd = jnp.concatenate([xg.astype(BF16)] * SSD_HPG, axis=0) * bmask_ref[...]
            st = s_ref[g]
            y = _dot(sc, bd) + _dot(cg, st) * jnp.exp(colb)
            s_ref[g] = st * jnp.exp(clast) + _dot_tn(bg, xg * jnp.exp(clast - colb))
            y = y + dsk_ref[:, gs] * xs
            y = y * _silu(z_ref[rs, gs])
            ms = jnp.mean(y * y, axis=-1, keepdims=True)
            yg_ref[rs, gs] = (y * lax.rsqrt(ms + RMS_EPS) * nw_ref[:, gs]).astype(yg_ref.dtype)

    def finish():
        sfin_ref[0] = s_ref[...]
        ufin_ref[0] = ext_ref[0:8, :]

    return init, chunk_step, finish


N_GLA_IN, N_GLA_OUT, N_GLA_SCRATCH = 10, 2, 1
N_SSD_IN, N_SSD_OUT = 16, 3


def _scan_kernel(*refs, cb, mask_rows):
    gla_in = refs[:N_GLA_IN]
    ssd_in = refs[N_GLA_IN:N_GLA_IN + N_SSD_IN]
    outs = refs[N_GLA_IN + N_SSD_IN:N_GLA_IN + N_SSD_IN + N_GLA_OUT + N_SSD_OUT]
    scratch = refs[N_GLA_IN + N_SSD_IN + N_GLA_OUT + N_SSD_OUT:]
    gla = _gla_parts(*gla_in, *outs[:N_GLA_OUT], *scratch[:N_GLA_SCRATCH], mask_rows=mask_rows)
    ssd = _ssd_parts(*ssd_in, *outs[N_GLA_OUT:], *scratch[N_GLA_SCRATCH:], mask_rows=mask_rows)
    c = pl.program_id(1)

    @pl.when(c == 0)
    def _():
        gla[0]()
        ssd[0]()

    def chunk_step(ci, carry):
        gla[1](ci)
        ssd[1](ci)
        return carry

    lax.fori_loop(0, cb, chunk_step, 0, unroll=True)

    @pl.when(c == pl.num_programs(1) - 1)
    def _():
        gla[2]()
        ssd[2]()


def _scan(proj, small, gla_w, gla_s0, ssd_w, ssd_s0, u0, consts, *, batch, seq, cb, mask_rows):
    nb = seq // (cb * CHUNK)
    rb = cb * CHUNK
    row = lambda b, c: b * nb + c
    w2p, bgate, gnw = gla_w
    cw, cbias, dtb, alog, dsk, snw = ssd_w
    tri, expand, psel, par, bmask = consts
    kern = functools.partial(_scan_kernel, cb=cb, mask_rows=mask_rows)
    full2 = lambda shape: pl.BlockSpec(shape, lambda b, c: (0, 0))
    full3 = lambda shape: pl.BlockSpec(shape, lambda b, c: (0, 0, 0))
    col = lambda width, off: pl.BlockSpec((rb, width), lambda b, c: (row(b, c), off // width))
    gla_specs = [
        col(GLA_QK, COL_Q), col(GLA_QK, COL_K), col(GLA_VW, COL_V), col(GLA_VW, COL_GOUT), col(SMALL_W, 0),
        full2((SMALL_W, GLA_QK)), full2((1, GLA_QK)), full2((1, GLA_DV)),
        full3((GLA_HEADS, GLA_DV, GLA_DK)), full2((CHUNK, CHUNK)),
    ]
    ssd_specs = [
        col(SSD_XBC, COL_XBC), col(SSD_D_INNER, COL_Z), col(SMALL_W, 0),
        full2((SSD_CONV, SSD_XBC)), full2((1, SSD_XBC)), full2((1, SMALL_W)), full2((1, SMALL_W)),
        full2((1, SSD_D_INNER)), full2((1, SSD_D_INNER)),
        full3((SSD_GROUPS, SSD_STATE, SSD_GW)), full2((8, SSD_XBC)), full2((CHUNK, CHUNK)),
        full2((SMALL_W, SSD_D_INNER)), full2((SSD_HEADS // 2, SMALL_W)), full2((2, SMALL_W)), full2((SSD_GW, SSD_GW)),
    ]
    assert len(gla_specs) == N_GLA_IN and len(ssd_specs) == N_SSD_IN
    return pl.pallas_call(
        kern,
        grid=(batch, nb),
        in_specs=gla_specs + ssd_specs,
        out_specs=[
            pl.BlockSpec((rb, GLA_VW), lambda b, c: (row(b, c), 0)),
            pl.BlockSpec((1, GLA_HEADS, GLA_DV, GLA_DK), lambda b, c: (b, 0, 0, 0)),
            pl.BlockSpec((rb, SSD_D_INNER), lambda b, c: (row(b, c), 0)),
            pl.BlockSpec((1, SSD_GROUPS, SSD_STATE, SSD_GW), lambda b, c: (b, 0, 0, 0)),
            pl.BlockSpec((1, 8, SSD_XBC), lambda b, c: (b, 0, 0)),
        ],
        out_shape=[
            jax.ShapeDtypeStruct((batch * seq, GLA_VW), BF16),
            jax.ShapeDtypeStruct((batch, GLA_HEADS, GLA_DV, GLA_DK), F32),
            jax.ShapeDtypeStruct((batch * seq, SSD_D_INNER), BF16),
            jax.ShapeDtypeStruct((batch, SSD_GROUPS, SSD_STATE, SSD_GW), F32),
            jax.ShapeDtypeStruct((batch, 8, SSD_XBC), F32),
        ],
        scratch_shapes=[pltpu.VMEM((GLA_HEADS, GLA_DV, GLA_DK), F32),
                        pltpu.VMEM((SSD_GROUPS, SSD_STATE, SSD_GW), F32),
                        pltpu.VMEM((CHUNK + 8, SSD_XBC), F32),
                        pltpu.VMEM((CHUNK, SSD_XBC), F32)],
        compiler_params=pltpu.CompilerParams(
            dimension_semantics=("arbitrary", "arbitrary"), vmem_limit_bytes=V7X_VMEM_LIMIT),
        name="gla_ssd_scan",
    )(proj, proj, proj, proj, small, w2p, bgate, gnw, gla_s0, tri,
      proj, proj, small, cw, cbias, dtb, alog, dsk, snw, ssd_s0, u0, tri, expand, psel, par, bmask)


def _merge_kernel(og_ref, yg_ref, ga_ref, gb_ref, x_ref, wa_ref, wb_ref, wo_ref, lnw_ref, h_ref, hn_ref):
    ya = jnp.dot(og_ref[...], wa_ref[...], preferred_element_type=F32)
    yb = jnp.dot(yg_ref[...], wb_ref[...], preferred_element_type=F32)
    mixed = _sigmoid(ga_ref[...]) * ya + _sigmoid(gb_ref[...]) * yb
    h = x_ref[...] + _dot(mixed, wo_ref[...])
    h_ref[...] = h
    ms = jnp.mean(h * h, axis=-1, keepdims=True)
    hn_ref[...] = (h * lax.rsqrt(ms + RMS_EPS) * lnw_ref[...]).astype(BF16)


def _merge(og, yg, proj, x2d, wa, wb, wo, lnw):
    rows = x2d.shape[0]
    tm = min(512, rows)
    full = lambda shape: pl.BlockSpec(shape, lambda i: (0, 0))
    return pl.pallas_call(
        _merge_kernel,
        grid=(rows // tm,),
        in_specs=[
            pl.BlockSpec((tm, GLA_VW), lambda i: (i, 0)),
            pl.BlockSpec((tm, SSD_D_INNER), lambda i: (i, 0)),
            pl.BlockSpec((tm, D_MODEL), lambda i: (i, COL_GA // D_MODEL)),
            pl.BlockSpec((tm, D_MODEL), lambda i: (i, COL_GB // D_MODEL)),
            pl.BlockSpec((tm, D_MODEL), lambda i: (i, 0)),
            full((GLA_VW, D_MODEL)),
            full((SSD_D_INNER, D_MODEL)),
            full((D_MODEL, D_MODEL)),
            full((1, D_MODEL)),
        ],
        out_specs=[
            pl.BlockSpec((tm, D_MODEL), lambda i: (i, 0)),
            pl.BlockSpec((tm, D_MODEL), lambda i: (i, 0)),
        ],
        out_shape=[
            jax.ShapeDtypeStruct((rows, D_MODEL), F32),
            jax.ShapeDtypeStruct((rows, D_MODEL), BF16),
        ],
        compiler_params=pltpu.CompilerParams(
            dimension_semantics=("arbitrary",), vmem_limit_bytes=V7X_VMEM_LIMIT),
        name="merge",
    )(og, yg, proj, proj, x2d, wa, wb, wo, lnw)


PEER_TN = 512
PEER_EB = 1024
PEER_CW = 128


def _topk_rows(vals, codes, k, write):
    n = vals.shape[0]
    tiles = [vals[t:t + 8] for t in range(0, n, 8)]
    ctiles = [codes[t:t + 8] for t in range(0, n, 8)]
    none8 = jnp.full((8, vals.shape[1]), _CODE_NONE, F32)

    def lowest_hit(ts, cs, m):
        c8 = none8
        for t, c in zip(ts[::-1], cs[::-1]):
            c8 = jnp.where(t == m, c, c8)
        return c8

    for r in range(k):
        level = tiles
        while len(level) > 1:
            level = [jnp.maximum(level[j], level[j + 1]) if j + 1 < len(level) else level[j]
                     for j in range(0, len(level), 2)]
        m = jnp.max(level[0], axis=0, keepdims=True)
        mid = len(tiles) // 2
        c8 = jnp.minimum(lowest_hit(tiles[:mid], ctiles[:mid], m), lowest_hit(tiles[mid:], ctiles[mid:], m))
        idx = jnp.min(c8, axis=0, keepdims=True)
        write(r, m, idx)
        tiles = [jnp.where(c == idx, -jnp.inf, t) for t, c in zip(tiles, ctiles)]


_CODE_NONE = float(1 << 20)
PEER_NB = (16, 8, 5, 4, 3, 2, 2, 2)
PEER_CAND_ROWS = 16 + 8 * 7 + 8


def _cand_codes(lanes):
    r = lax.broadcasted_iota(jnp.int32, (PEER_CAND_ROWS, lanes), 0)
    mid = r - 16
    code_mid = (((mid >> 3) + 1) << 4) + (mid & 7)
    code_last = (r - 72 + 8) << 4
    return jnp.where(r < 16, r, jnp.where(r < 72, code_mid, code_last)).astype(F32)


_GELU_C = math.sqrt(2.0 / math.pi)


def _gelu_tanh_times(x, half_w):
    inner = _GELU_C * (x + 0.044715 * (x * x * x))
    return (x * (1.0 + jnp.tanh(inner))) * half_w


def _select_scores(hn_ref, wq_ref, keys_ref):
    qt = lax.dot_general(wq_ref[...], hn_ref[...], (((1,), (1,)), ((), ())),
                         preferred_element_type=F32)
    return jnp.dot(keys_ref[0, 0], qt.astype(BF16), preferred_element_type=F32)


def _select_level1(e, s, tv_ref, ti_ref):
    half = e % 2
    key_codes = lax.broadcasted_iota(jnp.int32, (PEER_N_KEYS, PEER_CW), 0).astype(F32)
    for lb in range(s.shape[1] // PEER_CW):
        ls = slice(lb * PEER_CW, (lb + 1) * PEER_CW)

        def write1(r, m, idx, ls=ls):
            tv_ref[half, r:r + 1, ls] = m
            ti_ref[half, r:r + 1, ls] = idx

        _topk_rows(s[:, ls], key_codes, PEER_TOPK, write1)


def _select_level2(e, tv_ref, ti_ref, cand_ref, bv_ref, bp_ref, oi_ref, oj_ref, ow_ref):
    head = e // 2
    lanes = tv_ref.shape[2]

    @pl.when(e % 2 == 1)
    def _():
        cand_codes = _cand_codes(PEER_CW)
        sub8 = lax.broadcasted_iota(jnp.int32, (8, PEER_CW), 0)
        for lb in range(lanes // PEER_CW):
            ls = slice(lb * PEER_CW, (lb + 1) * PEER_CW)
            tv1 = tv_ref[1, :, ls]
            cand_ref[0:PEER_TOPK, ls] = tv_ref[0, 0:1, ls] + tv1
            for a in range(1, 8):
                row = 16 + 8 * (a - 1)
                cand_ref[row:row + 8, ls] = jnp.where(sub8 < PEER_NB[a], tv_ref[0, a:a + 1, ls] + tv1[0:8, :],
                                                      -jnp.inf)
            cand_ref[72:80, ls] = tv_ref[0, 8:PEER_TOPK, ls] + tv1[0:1, :]

            def write2(r, m, idx, ls=ls):
                bv_ref[r:r + 1, ls] = m
                bp_ref[r:r + 1, ls] = idx

            _topk_rows(cand_ref[:, ls], cand_codes, PEER_TOPK, write2)
        best = bv_ref[...]
        pos = bp_ref[...].astype(jnp.int32)
        a_sel = pos >> 4
        b_sel = pos & (PEER_TOPK - 1)
        isel = jnp.zeros_like(best)
        jsel = jnp.zeros_like(best)
        for a in range(PEER_TOPK):
            isel = jnp.where(a_sel == a, ti_ref[0, a:a + 1, :], isel)
            jsel = jnp.where(b_sel == a, ti_ref[1, a:a + 1, :], jsel)
        ex = jnp.exp(best - best[0:1, :])
        wgt = ex / jnp.sum(ex, axis=0, keepdims=True)
        hs = pl.ds(pl.multiple_of(head * PEER_TOPK, PEER_TOPK), PEER_TOPK)
        oi_ref[hs, :] = isel
        oj_ref[hs, :] = jsel
        ow_ref[hs, :] = wgt


def _peer_kernel(hn_e_ref, hn_s_ref, wq_ref, keys_ref, h_ref, u_ref, v_ref, lnw_ref, o_ref,
                 r_ref, tv_ref, ti_ref, cand_ref, bv_ref, bp_ref, oi_ref, oj_ref, ow_ref, it_ref, jt_ref, wt_ref,
                 *, tn, pitch):
    i = pl.program_id(0)
    e = pl.program_id(1)
    ipb = PEER_EB // PEER_N_KEYS

    @pl.when((i == 0) & (e == 0))
    def _():
        def zero_slab(k, carry):
            r_ref[pl.ds(pl.multiple_of(k * pitch, 8), pitch), :] = jnp.zeros((pitch, PEER_N_KEYS), F32)
            return carry

        lax.fori_loop(0, PEER_N_KEYS, zero_slab, 0)

    @pl.when((i > 0) & (e == 0))
    def _():
        it_ref[...] = oi_ref[...].T
        jt_ref[...] = oj_ref[...].T
        wt_ref[...] = ow_ref[...].T
        sub = lax.broadcasted_iota(jnp.int32, (PEER_N_KEYS, PEER_SLOTS), 0).astype(F32)

        def body(n, carry):
            irow = it_ref[pl.ds(n, 1), :]
            jrow = jt_ref[pl.ds(n, 1), :]
            wrow = 0.5 * wt_ref[pl.ds(n, 1), :]
            q = jnp.where(sub == irow, wrow, 0.0).astype(BF16)
            p_t = jnp.where(sub == jrow, 1.0, 0.0).T.astype(BF16)
            r = jnp.dot(q, p_t, preferred_element_type=F32)
            r_ref[pl.ds(n, PEER_N_KEYS, stride=pitch), :] = r
            return carry

        lax.fori_loop(0, tn, body, 0, unroll=64)

    @pl.when(e == 0)
    def _():
        o_ref[...] = jnp.zeros_like(o_ref)

    scores = _select_scores(hn_s_ref, wq_ref, keys_ref)

    act = lax.dot_general(hn_e_ref[...], u_ref[...], (((1,), (1,)), ((), ())),
                          preferred_element_type=F32)
    wblk = jnp.concatenate(
        [r_ref[pl.ds(pl.multiple_of((e * ipb + ii) * pitch, 8), tn), :] for ii in range(ipb)], axis=1)
    g = _gelu_tanh_times(act, wblk).astype(BF16)
    o_ref[...] += jnp.dot(g, v_ref[...], preferred_element_type=F32)

    _select_level1(e, scores, tv_ref, ti_ref)
    _select_level2(e, tv_ref, ti_ref, cand_ref, bv_ref, bp_ref, oi_ref, oj_ref, ow_ref)

    @pl.when(e == pl.num_programs(1) - 1)
    def _():
        h = h_ref[...] + o_ref[...]
        ms = jnp.mean(h * h, axis=-1, keepdims=True)
        o_ref[...] = h * lax.rsqrt(ms + RMS_EPS) * lnw_ref[...]


def _peer(hn2, h2, wq_t, keys, u_tab, v_tab, lnw):
    t = hn2.shape[0]
    tn = min(PEER_TN, t)
    nt = t // tn
    pitch = tn + 8
    n_exp = u_tab.shape[0]
    assert n_exp // PEER_EB == 2 * PEER_HEADS
    kern = functools.partial(_peer_kernel, tn=tn, pitch=pitch)
    prev = lambda i, e: (jnp.maximum(i - 1, 0), 0)
    return pl.pallas_call(
        kern,
        grid=(nt + 1, n_exp // PEER_EB),
        in_specs=[
            pl.BlockSpec((tn, D_MODEL), prev),
            pl.BlockSpec((tn, D_MODEL), lambda i, e: (jnp.minimum(i, nt - 1), 0)),
            pl.BlockSpec((PEER_HALF, D_MODEL), lambda i, e: (e, 0)),
            pl.BlockSpec((1, 1, PEER_N_KEYS, PEER_HALF), lambda i, e: (e // 2, e % 2, 0, 0)),
            pl.BlockSpec((tn, D_MODEL), prev),
            pl.BlockSpec((PEER_EB, D_MODEL), lambda i, e: (e, 0)),
            pl.BlockSpec((PEER_EB, D_MODEL), lambda i, e: (e, 0)),
            pl.BlockSpec((1, D_MODEL), lambda i, e: (0, 0)),
        ],
        out_specs=pl.BlockSpec((tn, D_MODEL), prev),
        out_shape=jax.ShapeDtypeStruct((t, D_MODEL), F32),
        scratch_shapes=[
            pltpu.VMEM((PEER_N_KEYS * pitch, PEER_N_KEYS), F32),
            pltpu.VMEM((2, PEER_TOPK, tn), F32),
            pltpu.VMEM((2, PEER_TOPK, tn), F32),
            pltpu.VMEM((PEER_CAND_ROWS, tn), F32),
            pltpu.VMEM((PEER_TOPK, tn), F32),
            pltpu.VMEM((PEER_TOPK, tn), F32),
            pltpu.VMEM((PEER_SLOTS, tn), F32),
            pltpu.VMEM((PEER_SLOTS, tn), F32),
            pltpu.VMEM((PEER_SLOTS, tn), F32),
            pltpu.VMEM((tn, PEER_SLOTS), F32),
            pltpu.VMEM((tn, PEER_SLOTS), F32),
            pltpu.VMEM((tn, PEER_SLOTS), F32),
        ],
        compiler_params=pltpu.CompilerParams(
            dimension_semantics=("arbitrary", "arbitrary"), vmem_limit_bytes=V7X_VMEM_LIMIT_EXPERTS),
        name="peer",
    )(hn2, hn2, wq_t, keys, h2, u_tab, v_tab, lnw)


def _constants():
    t = jnp.arange(CHUNK)
    tri = (t[None, :] <= t[:, None]).astype(BF16)
    lane = jnp.arange(SSD_D_INNER)
    small_lane = jnp.arange(SMALL_W)
    head_of_lane = small_lane - SMALL_DT
    is_head = (head_of_lane >= 0) & (head_of_lane < SSD_HEADS)
    expand = (small_lane[:, None] == SMALL_DT + lane[None, :] // SSD_HEADDIM).astype(BF16)
    psel = (is_head[None, :] & (head_of_lane[None, :] // 2 == jnp.arange(SSD_HEADS // 2)[:, None])).astype(BF16)
    par = (is_head[None, :] & (head_of_lane[None, :] % 2 == jnp.arange(2)[:, None])).astype(BF16)
    gl = jnp.arange(SSD_GW)
    bmask = (gl[:, None] // CHUNK == gl[None, :] // SSD_HEADDIM).astype(BF16)
    return tri, expand, psel, par, bmask


def _pack_in_weights(w_in):
    o = 0
    parts = {}
    for name, width in (("q", GLA_QK), ("k", GLA_QK), ("v", GLA_VW), ("gout", GLA_VW), ("glow", GLA_GATE_RANK),
                        ("z", SSD_D_INNER), ("xbc", SSD_XBC), ("dt", SSD_HEADS), ("ga", D_MODEL), ("gb", D_MODEL)):
        parts[name] = w_in[:, o:o + width]
        o += width
    main = jnp.concatenate([parts[n] for n in ("q", "k", "v", "z", "gout", "ga", "xbc", "gb")], axis=1)
    small = jnp.concatenate(
        [parts["glow"], parts["dt"], jnp.zeros((D_MODEL, SMALL_W - GLA_GATE_RANK - SSD_HEADS), w_in.dtype)], axis=1)
    return main.astype(BF16), small.astype(BF16)


def _pad_small(vec, offset):
    out = jnp.zeros((1, SMALL_W), F32)
    return out.at[0, offset:offset + vec.shape[0]].set(vec.astype(F32))


def kernel(x, meta_tokens, ln_mix_w, w_in, gla_w_gate2, gla_b_gate, gla_norm_w, ssd_conv_w, ssd_conv_b,
           ssd_dt_bias, ssd_a_log, ssd_d, ssd_norm_w, w_up_gla, w_up_ssd, w_out, ln_ffn_w,
           peer_w_q, peer_sub_keys, peer_u, peer_v, ln_final_w):
    bsz, seq, _ = x.shape
    assert seq % (4 * CHUNK) == 0
    l = 0
    consts = _constants()
    tri = consts[0]

    w_main, w_small = _pack_in_weights(w_in[l])
    lnw = ln_mix_w[l].reshape(1, D_MODEL)
    w2p = jnp.zeros((SMALL_W, GLA_QK), F32).at[SMALL_GLOW:SMALL_GLOW + GLA_GATE_RANK].set(gla_w_gate2[l]).astype(BF16)
    bgate = gla_b_gate[l].reshape(1, GLA_QK)
    gnw = gla_norm_w[l].reshape(1, GLA_DV)
    cw = ssd_conv_w[l]
    cbias = ssd_conv_b[l].reshape(1, SSD_XBC)
    dtb = _pad_small(ssd_dt_bias[l], SMALL_DT)
    alog = _pad_small(ssd_a_log[l], SMALL_DT)
    dsk = jnp.repeat(ssd_d[l].astype(F32), SSD_HEADDIM).reshape(1, SSD_D_INNER)
    snw = ssd_norm_w[l].reshape(1, SSD_D_INNER)

    meta_rows = jnp.concatenate([jnp.zeros((META_PAD, D_MODEL), F32), meta_tokens.astype(F32)], axis=0)
    proj_m, small_m = _inproj(meta_rows, lnw, w_main, w_small)
    gla_s0 = jnp.zeros((GLA_HEADS, GLA_DV, GLA_DK), F32)
    ssd_s0 = jnp.zeros((SSD_GROUPS, SSD_STATE, SSD_GW), F32)
    u0 = jnp.zeros((8, SSD_XBC), F32)
    gla_w = (w2p, bgate, gnw)
    ssd_w = (cw, cbias, dtb, alog, dsk, snw)
    _, gla_s1, _, ssd_s1, u1 = _scan(proj_m, small_m, gla_w, gla_s0, ssd_w, ssd_s0, u0, consts,
                                     batch=1, seq=CHUNK, cb=1, mask_rows=META_PAD)

    x2d = x.reshape(bsz * seq, D_MODEL)
    proj, small = _inproj(x2d, lnw, w_main, w_small)
    og, _, yg, _, _ = _scan(proj, small, gla_w, gla_s1[0], ssd_w, ssd_s1[0], u1[0], consts,
                            batch=bsz, seq=seq, cb=4, mask_rows=0)
    h2, hn2 = _merge(og, yg, proj, x2d, w_up_gla[l].astype(BF16), w_up_ssd[l].astype(BF16),
                     w_out[l].astype(BF16), ln_ffn_w[l].reshape(1, D_MODEL))

    wq_t = peer_w_q[l].T.astype(BF16)
    keys = peer_sub_keys[l].astype(BF16)
    out = _peer(hn2, h2, wq_t, keys, peer_u[l].astype(BF16), peer_v[l].astype(BF16),
                ln_final_w.reshape(1, D_MODEL))
    return out.reshape(bsz, seq, D_MODEL)
```

```python
import functools
import math

import jax
import jax.numpy as jnp
from jax import lax
from jax.experimental import pallas as pl
from jax.experimental.pallas import tpu as pltpu

F32 = jnp.float32
BF16 = jnp.bfloat16

D_MODEL = 1024
CHUNK = 64
N_META = 16
META_PAD = CHUNK - N_META
RMS_EPS = 1e-6
GLA_HEADS = 4
GLA_DK = 128
GLA_DV = 256
GLA_GATE_RANK = 16
GLA_GATE_NORM = 16.0
GLA_QK = 512
GLA_VW = 1024
SSD_D_INNER = 2048
SSD_HEADDIM = 64
SSD_HEADS = 32
SSD_GROUPS = 4
SSD_HPG = 8
SSD_STATE = 128
SSD_CONV = 4
SSD_BC = 512
SSD_XBC = 3072
SSD_GW = SSD_D_INNER // SSD_GROUPS
PEER_HEADS = 8
PEER_N_KEYS = 128
PEER_HALF = 128
PEER_TOPK = 16
PEER_SLOTS = PEER_HEADS * PEER_TOPK

COL_Q, COL_K, COL_V, COL_Z, COL_GOUT, COL_GA, COL_XBC, COL_GB = 0, 512, 1024, 2048, 4096, 5120, 6144, 9216
MAIN_W = 10240
SMALL_W = 128
SMALL_GLOW = 0
SMALL_DT = 16

V7X_VMEM_LIMIT = 56 * 1024 * 1024
V7X_VMEM_LIMIT_EXPERTS = 60 * 1024 * 1024


def _dot(a, b):
    return jnp.dot(a.astype(BF16), b.astype(BF16), preferred_element_type=F32)


def _dot_nt(a, b):
    return lax.dot_general(a.astype(BF16), b.astype(BF16), (((1,), (1,)), ((), ())),
                           preferred_element_type=F32)


def _dot_tn(a, b):
    return lax.dot_general(a.astype(BF16), b.astype(BF16), (((0,), (0,)), ((), ())),
                           preferred_element_type=F32)


def _split3(x):
    x1 = x.astype(BF16)
    r = x - x1.astype(F32)
    x2 = r.astype(BF16)
    x3 = (r - x2.astype(F32)).astype(BF16)
    return x1, x2, x3


def _dot_exact_l(m, x):
    x1, x2, x3 = _split3(x)
    return (jnp.dot(m, x1, preferred_element_type=F32) + jnp.dot(m, x2, preferred_element_type=F32)
            + jnp.dot(m, x3, preferred_element_type=F32))


def _sigmoid(x):
    return 0.5 * jnp.tanh(0.5 * x) + 0.5


def _silu(x):
    return x * _sigmoid(x)


def _softplus(x):
    return jnp.maximum(x, 0.0) + jnp.log1p(jnp.exp(-jnp.abs(x)))


def _inproj_kernel(x_ref, lnw_ref, w_ref, ws_ref, o_ref, os_ref, hn_ref):
    @pl.when(pl.program_id(1) == 0)
    def _():
        x = x_ref[...]
        ms = jnp.mean(x * x, axis=-1, keepdims=True)
        hn_ref[...] = (x * lax.rsqrt(ms + RMS_EPS) * lnw_ref[...]).astype(BF16)
        os_ref[...] = jnp.dot(hn_ref[...], ws_ref[...], preferred_element_type=F32)

    o_ref[...] = jnp.dot(hn_ref[...], w_ref[...], preferred_element_type=F32)


def _inproj(x2d, lnw, w_main, w_small):
    rows = x2d.shape[0]
    tm = min(1024, rows)
    tn = 2560
    return pl.pallas_call(
        _inproj_kernel,
        grid=(rows // tm, MAIN_W // tn),
        in_specs=[
            pl.BlockSpec((tm, D_MODEL), lambda i, j: (i, 0)),
            pl.BlockSpec((1, D_MODEL), lambda i, j: (0, 0)),
            pl.BlockSpec((D_MODEL, tn), lambda i, j: (0, j)),
            pl.BlockSpec((D_MODEL, SMALL_W), lambda i, j: (0, 0)),
        ],
        out_specs=[
            pl.BlockSpec((tm, tn), lambda i, j: (i, j)),
            pl.BlockSpec((tm, SMALL_W), lambda i, j: (i, 0)),
        ],
        out_shape=[
            jax.ShapeDtypeStruct((rows, MAIN_W), F32),
            jax.ShapeDtypeStruct((rows, SMALL_W), F32),
        ],
        scratch_shapes=[pltpu.VMEM((tm, D_MODEL), BF16)],
        compiler_params=pltpu.CompilerParams(
            dimension_semantics=("arbitrary", "arbitrary"), vmem_limit_bytes=V7X_VMEM_LIMIT),
        name="inproj",
    )(x2d, lnw, w_main, w_small)


def _gla_parts(q_ref, k_ref, v_ref, g_ref, sm_ref, w2_ref, bg_ref, nw_ref, s0_ref, tri_ref,
               og_ref, sfin_ref, s_ref, *, mask_rows):
    def init():
        s_ref[...] = s0_ref[...]

    tri = tri_ref[...]
    tri_b = (lax.broadcasted_iota(jnp.int32, (CHUNK, CHUNK), 1)
             <= lax.broadcasted_iota(jnp.int32, (CHUNK, CHUNK), 0))
    scale = GLA_DK ** -0.5

    def chunk_step(ci):
        rs = pl.ds(pl.multiple_of(ci * CHUNK, CHUNK), CHUNK)
        lg = _dot(sm_ref[rs, :], w2_ref[...]) + bg_ref[...]
        lg = (jnp.minimum(lg, 0.0) - jnp.log1p(jnp.exp(-jnp.abs(lg)))) * (1.0 / GLA_GATE_NORM)
        if mask_rows:
            keep = lax.broadcasted_iota(jnp.int32, (CHUNK, GLA_QK), 0) >= mask_rows
            lg = jnp.where(keep, lg, 0.0)
        gcum = _dot_exact_l(tri, lg)
        for h in range(GLA_HEADS):
            ks = slice(h * GLA_DK, (h + 1) * GLA_DK)
            vs = slice(h * GLA_DV, (h + 1) * GLA_DV)
            qh = q_ref[rs, ks] * scale
            kh = k_ref[rs, ks]
            vh = v_ref[rs, vs]
            if mask_rows:
                keep_k = lax.broadcasted_iota(jnp.int32, (CHUNK, GLA_DK), 0) >= mask_rows
                keep_v = lax.broadcasted_iota(jnp.int32, (CHUNK, GLA_DV), 0) >= mask_rows
                qh = jnp.where(keep_k, qh, 0.0)
                kh = jnp.where(keep_k, kh, 0.0)
                vh = jnp.where(keep_v, vh, 0.0)
            gh = gcum[:, ks]
            eg = jnp.exp(gh)
            egn = jnp.exp(-gh)
            qe = qh * eg
            a_causal = _dot_nt(qe, kh * egn)
            a_ahead = _dot_nt(qh * egn, kh * eg)
            att = jnp.where(tri_b, a_causal, a_ahead)
            st = s_ref[h]
            o = _dot(att, vh) + _dot_nt(qe, st)
            glast = gh[CHUNK - 1:CHUNK, :]
            kd = kh * jnp.exp(glast - gh)
            s_ref[h] = st * jnp.exp(glast) + _dot_tn(vh, kd)
            ms = jnp.mean(o * o, axis=-1, keepdims=True)
            on = o * lax.rsqrt(ms + RMS_EPS) * nw_ref[...]
            og_ref[rs, vs] = (on * _silu(g_ref[rs, vs])).astype(og_ref.dtype)

    def finish():
        sfin_ref[0] = s_ref[...]

    return init, chunk_step, finish


def _ssd_parts(xbc_ref, z_ref, sm_ref, cw_ref, cbias_ref, dtb_ref, alog_ref, dsk_ref, nw_ref,
               s0_ref, u0_ref, tri_ref, exp_ref, psel_ref, par_ref, bmask_ref,
               yg_ref, sfin_ref, ufin_ref, s_ref, ext_ref, xc_ref, *, mask_rows):
    def init():
        s_ref[...] = s0_ref[...]
        ext_ref[0:8, :] = u0_ref[...]

    tri = tri_ref[...]
    psel = psel_ref[...]
    par0 = par_ref[0:1, :]
    par1 = par_ref[1:2, :]
    lane_blk = 512

    def chunk_step(ci):
        rs = pl.ds(pl.multiple_of(ci * CHUNK, CHUNK), CHUNK)
        for lb in range(SSD_XBC // lane_blk):
            ls = slice(lb * lane_blk, (lb + 1) * lane_blk)
            ext_ref[8:8 + CHUNK, ls] = xbc_ref[rs, ls]
            conv = cbias_ref[:, ls] + ext_ref[8:8 + CHUNK, ls] * cw_ref[SSD_CONV - 1:SSD_CONV, ls]
            for j in range(1, SSD_CONV):
                conv = conv + ext_ref[8 - j:8 - j + CHUNK, ls] * cw_ref[SSD_CONV - 1 - j:SSD_CONV - j, ls]
            xc = _silu(conv)
            if mask_rows:
                xc = jnp.where(lax.broadcasted_iota(jnp.int32, (CHUNK, lane_blk), 0) >= mask_rows, xc, 0.0)
            xc_ref[:, ls] = xc
            ext_ref[0:8, ls] = ext_ref[CHUNK:CHUNK + 8, ls]
        dt = _softplus(sm_ref[rs, :] + dtb_ref[...])
        if mask_rows:
            dt = jnp.where(lax.broadcasted_iota(jnp.int32, (CHUNK, SMALL_W), 0) >= mask_rows, dt, 0.0)
        a = -dt * jnp.exp(alog_ref[...])
        acum = _dot_exact_l(tri, a)
        acum_parts = _split3(acum)
        parts6 = jnp.concatenate(acum_parts + _split3(dt), axis=0)
        acum_t = sum(
            lax.dot_general(psel, jnp.concatenate([x * par0, x * par1], axis=0), (((1,), (1,)), ((), ())),
                            preferred_element_type=F32) for x in acum_parts)
        for g in range(SSD_GROUPS):
            gs = slice(g * SSD_GW, (g + 1) * SSD_GW)
            expand = exp_ref[:, gs]
            ex = jnp.dot(parts6, expand, preferred_element_type=F32)
            colb = ex[0:CHUNK] + ex[CHUNK:2 * CHUNK] + ex[2 * CHUNK:3 * CHUNK]
            dtx = ex[3 * CHUNK:4 * CHUNK] + ex[4 * CHUNK:5 * CHUNK] + ex[5 * CHUNK:]
            rowb = jnp.concatenate(
                [jnp.broadcast_to(acum_t[4 * g + k:4 * g + k + 1, :], (CHUNK, 2 * CHUNK)) for k in range(4)],
                axis=1)
            seg = jnp.exp(-jnp.abs(colb - rowb))
            clast = colb[CHUNK - 1:CHUNK, :]
            xs = xc_ref[:, gs]
            bg = xc_ref[:, SSD_D_INNER + g * SSD_STATE:SSD_D_INNER + (g + 1) * SSD_STATE]
            cg = xc_ref[:, SSD_D_INNER + SSD_BC + g * SSD_STATE:SSD_D_INNER + SSD_BC + (g + 1) * SSD_STATE]
            xg = xs * dtx
            cbx = _dot_nt(cg, jnp.concatenate([bg] * SSD_HPG, axis=0))
            sc = seg * cbx
            bd = jnp.concatenate([xg.astype(BF16)] * SSD_HPG, axis=0) * bmask_ref[...]
            st = s_ref[g]
            y = _dot(sc, bd) + _dot(cg, st) * jnp.exp(colb)
            s_ref[g] = st * jnp.exp(clast) + _dot_tn(bg, xg * jnp.exp(clast - colb))
            y = y + dsk_ref[:, gs] * xs
            y = y * _silu(z_ref[rs, gs])
            ms = jnp.mean(y * y, axis=-1, keepdims=True)
            yg_ref[rs, gs] = (y * lax.rsqrt(ms + RMS_EPS) * nw_ref[:, gs]).astype(yg_ref.dtype)

    def finish():
        sfin_ref[0] = s_ref[...]
        ufin_ref[0] = ext_ref[0:8, :]

    return init, chunk_step, finish


N_GLA_IN, N_GLA_OUT, N_GLA_SCRATCH = 10, 2, 1
N_SSD_IN, N_SSD_OUT = 16, 3


def _scan_kernel(*refs, cb, mask_rows):
    gla_in = refs[:N_GLA_IN]
    ssd_in = refs[N_GLA_IN:N_GLA_IN + N_SSD_IN]
    outs = refs[N_GLA_IN + N_SSD_IN:N_GLA_IN + N_SSD_IN + N_GLA_OUT + N_SSD_OUT]
    scratch = refs[N_GLA_IN + N_SSD_IN + N_GLA_OUT + N_SSD_OUT:]
    gla = _gla_parts(*gla_in, *outs[:N_GLA_OUT], *scratch[:N_GLA_SCRATCH], mask_rows=mask_rows)
    ssd = _ssd_parts(*ssd_in, *outs[N_GLA_OUT:], *scratch[N_GLA_SCRATCH:], mask_rows=mask_rows)
    c = pl.program_id(1)

    @pl.when(c == 0)
    def _():
        gla[0]()
        ssd[0]()

    def chunk_step(ci, carry):
        gla[1](ci)
        ssd[1](ci)
        return carry

    lax.fori_loop(0, cb, chunk_step, 0, unroll=True)

    @pl.when(c == pl.num_programs(1) - 1)
    def _():
        gla[2]()
        ssd[2]()


def _scan(proj, small, gla_w, gla_s0, ssd_w, ssd_s0, u0, consts, *, batch, seq, cb, mask_rows):
    nb = seq // (cb * CHUNK)
    rb = cb * CHUNK
    row = lambda b, c: b * nb + c
    w2p, bgate, gnw = gla_w
    cw, cbias, dtb, alog, dsk, snw = ssd_w
    tri, expand, psel, par, bmask = consts
    kern = functools.partial(_scan_kernel, cb=cb, mask_rows=mask_rows)
    full2 = lambda shape: pl.BlockSpec(shape, lambda b, c: (0, 0))
    full3 = lambda shape: pl.BlockSpec(shape, lambda b, c: (0, 0, 0))
    col = lambda width, off: pl.BlockSpec((rb, width), lambda b, c: (row(b, c), off // width))
    gla_specs = [
        col(GLA_QK, COL_Q), col(GLA_QK, COL_K), col(GLA_VW, COL_V), col(GLA_VW, COL_GOUT), col(SMALL_W, 0),
        full2((SMALL_W, GLA_QK)), full2((1, GLA_QK)), full2((1, GLA_DV)),
        full3((GLA_HEADS, GLA_DV, GLA_DK)), full2((CHUNK, CHUNK)),
    ]
    ssd_specs = [
        col(SSD_XBC, COL_XBC), col(SSD_D_INNER, COL_Z), col(SMALL_W, 0),
        full2((SSD_CONV, SSD_XBC)), full2((1, SSD_XBC)), full2((1, SMALL_W)), full2((1, SMALL_W)),
        full2((1, SSD_D_INNER)), full2((1, SSD_D_INNER)),
        full3((SSD_GROUPS, SSD_STATE, SSD_GW)), full2((8, SSD_XBC)), full2((CHUNK, CHUNK)),
        full2((SMALL_W, SSD_D_INNER)), full2((SSD_HEADS // 2, SMALL_W)), full2((2, SMALL_W)), full2((SSD_GW, SSD_GW)),
    ]
    assert len(gla_specs) == N_GLA_IN and len(ssd_specs) == N_SSD_IN
    return pl.pallas_call(
        kern,
        grid=(batch, nb),
        in_specs=gla_specs + ssd_specs,
        out_specs=[
            pl.BlockSpec((rb, GLA_VW), lambda b, c: (row(b, c), 0)),
            pl.BlockSpec((1, GLA_HEADS, GLA_DV, GLA_DK), lambda b, c: (b, 0, 0, 0)),
            pl.BlockSpec((rb, SSD_D_INNER), lambda b, c: (row(b, c), 0)),
            pl.BlockSpec((1, SSD_GROUPS, SSD_STATE, SSD_GW), lambda b, c: (b, 0, 0, 0)),
            pl.BlockSpec((1, 8, SSD_XBC), lambda b, c: (b, 0, 0)),
        ],
        out_shape=[
            jax.ShapeDtypeStruct((batch * seq, GLA_VW), BF16),
            jax.ShapeDtypeStruct((batch, GLA_HEADS, GLA_DV, GLA_DK), F32),
            jax.ShapeDtypeStruct((batch * seq, SSD_D_INNER), BF16),
            jax.ShapeDtypeStruct((batch, SSD_GROUPS, SSD_STATE, SSD_GW), F32),
            jax.ShapeDtypeStruct((batch, 8, SSD_XBC), F32),
        ],
        scratch_shapes=[pltpu.VMEM((GLA_HEADS, GLA_DV, GLA_DK), F32),
                        pltpu.VMEM((SSD_GROUPS, SSD_STATE, SSD_GW), F32),
                        pltpu.VMEM((CHUNK + 8, SSD_XBC), F32),
                        pltpu.VMEM((CHUNK, SSD_XBC), F32)],
        compiler_params=pltpu.CompilerParams(
            dimension_semantics=("arbitrary", "arbitrary"), vmem_limit_bytes=V7X_VMEM_LIMIT),
        name="gla_ssd_scan",
    )(proj, proj, proj, proj, small, w2p, bgate, gnw, gla_s0, tri,
      proj, proj, small, cw, cbias, dtb, alog, dsk, snw, ssd_s0, u0, tri, expand, psel, par, bmask)


def _merge_kernel(og_ref, yg_ref, ga_ref, gb_ref, x_ref, wa_ref, wb_ref, wo_ref, lnw_ref, h_ref, hn_ref):
    ya = jnp.dot(og_ref[...], wa_ref[...], preferred_element_type=F32)
    yb = jnp.dot(yg_ref[...], wb_ref[...], preferred_element_type=F32)
    mixed = _sigmoid(ga_ref[...]) * ya + _sigmoid(gb_ref[...]) * yb
    h = x_ref[...] + _dot(mixed, wo_ref[...])
    h_ref[...] = h
    ms = jnp.mean(h * h, axis=-1, keepdims=True)
    hn_ref[...] = (h * lax.rsqrt(ms + RMS_EPS) * lnw_ref[...]).astype(BF16)


def _merge(og, yg, proj, x2d, wa, wb, wo, lnw):
    rows = x2d.shape[0]
    tm = min(512, rows)
    full = lambda shape: pl.BlockSpec(shape, lambda i: (0, 0))
    return pl.pallas_call(
        _merge_kernel,
        grid=(rows // tm,),
        in_specs=[
            pl.BlockSpec((tm, GLA_VW), lambda i: (i, 0)),
            pl.BlockSpec((tm, SSD_D_INNER), lambda i: (i, 0)),
            pl.BlockSpec((tm, D_MODEL), lambda i: (i, COL_GA // D_MODEL)),
            pl.BlockSpec((tm, D_MODEL), lambda i: (i, COL_GB // D_MODEL)),
            pl.BlockSpec((tm, D_MODEL), lambda i: (i, 0)),
            full((GLA_VW, D_MODEL)),
            full((SSD_D_INNER, D_MODEL)),
            full((D_MODEL, D_MODEL)),
            full((1, D_MODEL)),
        ],
        out_specs=[
            pl.BlockSpec((tm, D_MODEL), lambda i: (i, 0)),
            pl.BlockSpec((tm, D_MODEL), lambda i: (i, 0)),
        ],
        out_shape=[
            jax.ShapeDtypeStruct((rows, D_MODEL), F32),
            jax.ShapeDtypeStruct((rows, D_MODEL), BF16),
        ],
        compiler_params=pltpu.CompilerParams(
            dimension_semantics=("arbitrary",), vmem_limit_bytes=V7X_VMEM_LIMIT),
        name="merge",
    )(og, yg, proj, proj, x2d, wa, wb, wo, lnw)


PEER_TN = 512
PEER_EB = 1024
PEER_CW = 128


def _topk_rows(vals, codes, k, write):
    n = vals.shape[0]
    tiles = [vals[t:t + 8] for t in range(0, n, 8)]
    ctiles = [codes[t:t + 8] for t in range(0, n, 8)]
    none8 = jnp.full((8, vals.shape[1]), _CODE_NONE, F32)

    def lowest_hit(ts, cs, m):
        c8 = none8
        for t, c in zip(ts[::-1], cs[::-1]):
            c8 = jnp.where(t == m, c, c8)
        return c8

    for r in range(k):
        level = tiles
        while len(level) > 1:
            level = [jnp.maximum(level[j], level[j + 1]) if j + 1 < len(level) else level[j]
                     for j in range(0, len(level), 2)]
        m = jnp.max(level[0], axis=0, keepdims=True)
        mid = len(tiles) // 2
        c8 = jnp.minimum(lowest_hit(tiles[:mid], ctiles[:mid], m), lowest_hit(tiles[mid:], ctiles[mid:], m))
        idx = jnp.min(c8, axis=0, keepdims=True)
        write(r, m, idx)
        tiles = [jnp.where(c == idx, -jnp.inf, t) for t, c in zip(tiles, ctiles)]


_CODE_NONE = float(1 << 20)
PEER_CAND_ROWS = 56
_CODE_PAD = 1000.0


def _cand_codes(lanes):
    r = lax.broadcasted_iota(jnp.int32, (8, lanes), 0)
    tiles = [
        r, 8 + r, 16 + r,
        jnp.where(r < 5, 32 + r, 64 + (r - 5)),
        jnp.where(r < 4, 48 + r, jnp.where(r < 6, 80 + (r - 4), 96 + (r - 6))),
        jnp.where(r < 2, 112 + r, (8 + (r - 2)) * 16),
        jnp.where(r < 2, (14 + r) * 16, int(_CODE_PAD) + r),
    ]
    return jnp.concatenate(tiles, axis=0).astype(F32)


def _cand_values(tv0, tv1):
    r = lax.broadcasted_iota(jnp.int32, (8, tv0.shape[1]), 0)
    t1 = tv1[0:8]
    hi = pltpu.roll(tv0[8:16], 2, 0) + tv1[0:1]
    tiles = [
        tv0[0:1] + t1, tv0[0:1] + tv1[8:16], tv0[1:2] + t1,
        jnp.where(r < 5, tv0[2:3] + t1, tv0[4:5] + pltpu.roll(t1, 5, 0)),
        jnp.where(r < 4, tv0[3:4] + t1, jnp.where(r < 6, tv0[5:6] + pltpu.roll(t1, 4, 0), tv0[6:7] + pltpu.roll(t1, 6, 0))),
        jnp.where(r < 2, tv0[7:8] + t1, hi),
        jnp.where(r < 2, hi, -jnp.inf),
    ]
    return jnp.concatenate(tiles, axis=0)


_GELU_C = math.sqrt(2.0 / math.pi)


def _gelu_tanh_times(x, half_w):
    inner = _GELU_C * (x + 0.044715 * (x * x * x))
    return (x * (1.0 + jnp.tanh(inner))) * half_w


def _select_scores(hn_ref, wq_ref, keys_ref):
    qt = lax.dot_general(wq_ref[...], hn_ref[...], (((1,), (1,)), ((), ())),
                         preferred_element_type=F32)
    return jnp.dot(keys_ref[0, 0], qt.astype(BF16), preferred_element_type=F32)


def _select_level1(e, s, tv_ref, ti_ref):
    half = e % 2
    key_codes = lax.broadcasted_iota(jnp.int32, (PEER_N_KEYS, PEER_CW), 0).astype(F32)
    for lb in range(s.shape[1] // PEER_CW):
        ls = slice(lb * PEER_CW, (lb + 1) * PEER_CW)

        def write1(r, m, idx, ls=ls):
            tv_ref[half, r:r + 1, ls] = m
            ti_ref[half, r:r + 1, ls] = idx

        _topk_rows(s[:, ls], key_codes, PEER_TOPK, write1)


def _select_level2(e, tv_ref, ti_ref, cand_ref, bv_ref, bp_ref, oi_ref, oj_ref, ow_ref):
    head = e // 2
    lanes = tv_ref.shape[2]

    @pl.when(e % 2 == 1)
    def _():
        cand_codes = _cand_codes(PEER_CW)
        for lb in range(lanes // PEER_CW):
            ls = slice(lb * PEER_CW, (lb + 1) * PEER_CW)
            cand_ref[:, ls] = _cand_values(tv_ref[0, :, ls], tv_ref[1, :, ls])

            def write2(r, m, idx, ls=ls):
                bv_ref[r:r + 1, ls] = m
                bp_ref[r:r + 1, ls] = idx

            _topk_rows(cand_ref[:, ls], cand_codes, PEER_TOPK, write2)
        best = bv_ref[...]
        pos = bp_ref[...].astype(jnp.int32)
        a_sel = pos >> 4
        b_sel = pos & (PEER_TOPK - 1)
        isel = jnp.zeros_like(best)
        jsel = jnp.zeros_like(best)
        for a in range(PEER_TOPK):
            isel = jnp.where(a_sel == a, ti_ref[0, a:a + 1, :], isel)
            jsel = jnp.where(b_sel == a, ti_ref[1, a:a + 1, :], jsel)
        ex = jnp.exp(best - best[0:1, :])
        wgt = ex / jnp.sum(ex, axis=0, keepdims=True)
        hs = pl.ds(pl.multiple_of(head * PEER_TOPK, PEER_TOPK), PEER_TOPK)
        oi_ref[hs, :] = isel
        oj_ref[hs, :] = jsel
        ow_ref[hs, :] = wgt


def _peer_kernel(hn_e_ref, hn_s_ref, wq_ref, keys_ref, h_ref, u_ref, v_ref, lnw_ref, o_ref,
                 r_ref, tv_ref, ti_ref, cand_ref, bv_ref, bp_ref, oi_ref, oj_ref, ow_ref, it_ref, jt_ref, wt_ref,
                 *, tn, pitch):
    i = pl.program_id(0)
    e = pl.program_id(1)
    ipb = PEER_EB // PEER_N_KEYS

    @pl.when((i == 0) & (e == 0))
    def _():
        def zero_slab(k, carry):
            r_ref[pl.ds(pl.multiple_of(k * pitch, 8), pitch), :] = jnp.zeros((pitch, PEER_N_KEYS), F32)
            return carry

        lax.fori_loop(0, PEER_N_KEYS, zero_slab, 0)

    @pl.when((i > 0) & (e == 0))
    def _():
        it_ref[...] = oi_ref[...].T
        jt_ref[...] = oj_ref[...].T
        wt_ref[...] = ow_ref[...].T
        sub = lax.broadcasted_iota(jnp.int32, (PEER_N_KEYS, PEER_SLOTS), 0).astype(F32)

        def body(n, carry):
            irow = it_ref[pl.ds(n, 1), :]
            jrow = jt_ref[pl.ds(n, 1), :]
            wrow = 0.5 * wt_ref[pl.ds(n, 1), :]
            q = jnp.where(sub == irow, wrow, 0.0).astype(BF16)
            p_t = jnp.where(sub == jrow, 1.0, 0.0).T.astype(BF16)
            r = jnp.dot(q, p_t, preferred_element_type=F32)
            r_ref[pl.ds(n, PEER_N_KEYS, stride=pitch), :] = r
            return carry

        lax.fori_loop(0, tn, body, 0, unroll=64)

    @pl.when(e == 0)
    def _():
        o_ref[...] = jnp.zeros_like(o_ref)

    scores = _select_scores(hn_s_ref, wq_ref, keys_ref)

    act = lax.dot_general(hn_e_ref[...], u_ref[...], (((1,), (1,)), ((), ())),
                          preferred_element_type=F32)
    wblk = jnp.concatenate(
        [r_ref[pl.ds(pl.multiple_of((e * ipb + ii) * pitch, 8), tn), :] for ii in range(ipb)], axis=1)
    g = _gelu_tanh_times(act, wblk).astype(BF16)
    o_ref[...] += jnp.dot(g, v_ref[...], preferred_element_type=F32)

    _select_level1(e, scores, tv_ref, ti_ref)
    _select_level2(e, tv_ref, ti_ref, cand_ref, bv_ref, bp_ref, oi_ref, oj_ref, ow_ref)

    @pl.when(e == pl.num_programs(1) - 1)
    def _():
        h = h_ref[...] + o_ref[...]
        ms = jnp.mean(h * h, axis=-1, keepdims=True)
        o_ref[...] = h * lax.rsqrt(ms + RMS_EPS) * lnw_ref[...]


def _peer(hn2, h2, wq_t, keys, u_tab, v_tab, lnw):
    t = hn2.shape[0]
    tn = min(PEER_TN, t)
    nt = t // tn
    pitch = tn + 8
    n_exp = u_tab.shape[0]
    assert n_exp // PEER_EB == 2 * PEER_HEADS
    kern = functools.partial(_peer_kernel, tn=tn, pitch=pitch)
    prev = lambda i, e: (jnp.maximum(i - 1, 0), 0)
    return pl.pallas_call(
        kern,
        grid=(nt + 1, n_exp // PEER_EB),
        in_specs=[
            pl.BlockSpec((tn, D_MODEL), prev),
            pl.BlockSpec((tn, D_MODEL), lambda i, e: (jnp.minimum(i, nt - 1), 0)),
            pl.BlockSpec((PEER_HALF, D_MODEL), lambda i, e: (e, 0)),
            pl.BlockSpec((1, 1, PEER_N_KEYS, PEER_HALF), lambda i, e: (e // 2, e % 2, 0, 0)),
            pl.BlockSpec((tn, D_MODEL), prev),
            pl.BlockSpec((PEER_EB, D_MODEL), lambda i, e: (e, 0)),
            pl.BlockSpec((PEER_EB, D_MODEL), lambda i, e: (e, 0)),
            pl.BlockSpec((1, D_MODEL), lambda i, e: (0, 0)),
        ],
        out_specs=pl.BlockSpec((tn, D_MODEL), prev),
        out_shape=jax.ShapeDtypeStruct((t, D_MODEL), F32),
        scratch_shapes=[
            pltpu.VMEM((PEER_N_KEYS * pitch, PEER_N_KEYS), F32),
            pltpu.VMEM((2, PEER_TOPK, tn), F32),
            pltpu.VMEM((2, PEER_TOPK, tn), F32),
            pltpu.VMEM((PEER_CAND_ROWS, tn), F32),
            pltpu.VMEM((PEER_TOPK, tn), F32),
            pltpu.VMEM((PEER_TOPK, tn), F32),
            pltpu.VMEM((PEER_SLOTS, tn), F32),
            pltpu.VMEM((PEER_SLOTS, tn), F32),
            pltpu.VMEM((PEER_SLOTS, tn), F32),
            pltpu.VMEM((tn, PEER_SLOTS), F32),
            pltpu.VMEM((tn, PEER_SLOTS), F32),
            pltpu.VMEM((tn, PEER_SLOTS), F32),
        ],
        compiler_params=pltpu.CompilerParams(
            dimension_semantics=("arbitrary", "arbitrary"), vmem_limit_bytes=V7X_VMEM_LIMIT_EXPERTS),
        name="peer",
    )(hn2, hn2, wq_t, keys, h2, u_tab, v_tab, lnw)


def _constants():
    t = jnp.arange(CHUNK)
    tri = (t[None, :] <= t[:, None]).astype(BF16)
    lane = jnp.arange(SSD_D_INNER)
    small_lane = jnp.arange(SMALL_W)
    head_of_lane = small_lane - SMALL_DT
    is_head = (head_of_lane >= 0) & (head_of_lane < SSD_HEADS)
    expand = (small_lane[:, None] == SMALL_DT + lane[None, :] // SSD_HEADDIM).astype(BF16)
    psel = (is_head[None, :] & (head_of_lane[None, :] // 2 == jnp.arange(SSD_HEADS // 2)[:, None])).astype(BF16)
    par = (is_head[None, :] & (head_of_lane[None, :] % 2 == jnp.arange(2)[:, None])).astype(BF16)
    gl = jnp.arange(SSD_GW)
    bmask = (gl[:, None] // CHUNK == gl[None, :] // SSD_HEADDIM).astype(BF16)
    return tri, expand, psel, par, bmask


def _pack_in_weights(w_in):
    o = 0
    parts = {}
    for name, width in (("q", GLA_QK), ("k", GLA_QK), ("v", GLA_VW), ("gout", GLA_VW), ("glow", GLA_GATE_RANK),
                        ("z", SSD_D_INNER), ("xbc", SSD_XBC), ("dt", SSD_HEADS), ("ga", D_MODEL), ("gb", D_MODEL)):
        parts[name] = w_in[:, o:o + width]
        o += width
    main = jnp.concatenate([parts[n] for n in ("q", "k", "v", "z", "gout", "ga", "xbc", "gb")], axis=1)
    small = jnp.concatenate(
        [parts["glow"], parts["dt"], jnp.zeros((D_MODEL, SMALL_W - GLA_GATE_RANK - SSD_HEADS), w_in.dtype)], axis=1)
    return main.astype(BF16), small.astype(BF16)


def _pad_small(vec, offset):
    out = jnp.zeros((1, SMALL_W), F32)
    return out.at[0, offset:offset + vec.shape[0]].set(vec.astype(F32))


def kernel(x, meta_tokens, ln_mix_w, w_in, gla_w_gate2, gla_b_gate, gla_norm_w, ssd_conv_w, ssd_conv_b,
           ssd_dt_bias, ssd_a_log, ssd_d, ssd_norm_w, w_up_gla, w_up_ssd, w_out, ln_ffn_w,
           peer_w_q, peer_sub_keys, peer_u, peer_v, ln_final_w):
    bsz, seq, _ = x.shape
    assert seq % (4 * CHUNK) == 0
    l = 0
    consts = _constants()
    tri = consts[0]

    w_main, w_small = _pack_in_weights(w_in[l])
    lnw = ln_mix_w[l].reshape(1, D_MODEL)
    w2p = jnp.zeros((SMALL_W, GLA_QK), F32).at[SMALL_GLOW:SMALL_GLOW + GLA_GATE_RANK].set(gla_w_gate2[l]).astype(BF16)
    bgate = gla_b_gate[l].reshape(1, GLA_QK)
    gnw = gla_norm_w[l].reshape(1, GLA_DV)
    cw = ssd_conv_w[l]
    cbias = ssd_conv_b[l].reshape(1, SSD_XBC)
    dtb = _pad_small(ssd_dt_bias[l], SMALL_DT)
    alog = _pad_small(ssd_a_log[l], SMALL_DT)
    dsk = jnp.repeat(ssd_d[l].astype(F32), SSD_HEADDIM).reshape(1, SSD_D_INNER)
    snw = ssd_norm_w[l].reshape(1, SSD_D_INNER)

    meta_rows = jnp.concatenate([jnp.zeros((META_PAD, D_MODEL), F32), meta_tokens.astype(F32)], axis=0)
    proj_m, small_m = _inproj(meta_rows, lnw, w_main, w_small)
    gla_s0 = jnp.zeros((GLA_HEADS, GLA_DV, GLA_DK), F32)
    ssd_s0 = jnp.zeros((SSD_GROUPS, SSD_STATE, SSD_GW), F32)
    u0 = jnp.zeros((8, SSD_XBC), F32)
    gla_w = (w2p, bgate, gnw)
    ssd_w = (cw, cbias, dtb, alog, dsk, snw)
    _, gla_s1, _, ssd_s1, u1 = _scan(proj_m, small_m, gla_w, gla_s0, ssd_w, ssd_s0, u0, consts,
                                     batch=1, seq=CHUNK, cb=1, mask_rows=META_PAD)

    x2d = x.reshape(bsz * seq, D_MODEL)
    proj, small = _inproj(x2d, lnw, w_main, w_small)
    og, _, yg, _, _ = _scan(proj, small, gla_w, gla_s1[0], ssd_w, ssd_s1[0], u1[0], consts,
                            batch=bsz, seq=seq, cb=4, mask_rows=0)
    h2, hn2 = _merge(og, yg, proj, x2d, w_up_gla[l].astype(BF16), w_up_ssd[l].astype(BF16),
                     w_out[l].astype(BF16), ln_ffn_w[l].reshape(1, D_MODEL))

    wq_t = peer_w_q[l].T.astype(BF16)
    keys = peer_sub_keys[l].astype(BF16)
    out = _peer(hn2, h2, wq_t, keys, peer_u[l].astype(BF16), peer_v[l].astype(BF16),
                ln_final_w.reshape(1, D_MODEL))
    return out.reshape(bsz, seq, D_MODEL)
```
